```python
import jax, jax.numpy as jnp
from jax import lax
import numpy as np

D_MODEL = 4096
BATCH = 4
SEQ = 4096
DEPTH = 1

POOL_GROUPS = 4
POOL_WINDOWS = (2, 4, 8, 16)
POOL_GROUP_DIM = D_MODEL // 8
POOL_WIDTH = POOL_GROUPS * POOL_GROUP_DIM
HEAD_DIM = 128
ATTN_HEADS = D_MODEL // (2 * HEAD_DIM)
ATTN_WIDTH = ATTN_HEADS * HEAD_DIM
MOBA_BLOCK = 256
MOBA_TOPK = 3
Q_CHUNK = 16
ROPE_THETA = 10000.0
IN_WIDTH = POOL_WIDTH + 3 * ATTN_WIDTH + 2 * D_MODEL
PEER_HEADS = 8
PEER_NKEYS = 128
PEER_EXPERTS = PEER_NKEYS * PEER_NKEYS
PEER_QDIM = 256
PEER_HALF = PEER_QDIM // 2
PEER_TOPK = 16
PEER_CHUNK = 32
EPS = 1e-6

kernel_name = "hybrid_pool_moba_peer_block"


def rmsnorm(x, g):
    xf = x.astype(jnp.float32)
    r = lax.rsqrt(jnp.mean(xf * xf, axis=-1, keepdims=True) + EPS)
    return (xf * r * g.astype(jnp.float32)).astype(x.dtype)


def rope(t, pos):
    half = HEAD_DIM // 2
    inv = ROPE_THETA ** (-jnp.arange(half, dtype=jnp.float32) / half)
    ang = pos.astype(jnp.float32)[:, None] * inv[None, :]
    cos, sin = jnp.cos(ang), jnp.sin(ang)
    t1 = t[..., :half].astype(jnp.float32)
    t2 = t[..., half:].astype(jnp.float32)
    out = jnp.concatenate([t1 * cos - t2 * sin, t2 * cos + t1 * sin], axis=-1)
    return out.astype(t.dtype)


def pool_mixer(xp, pool_w, pool_scale):
    B, S, _ = xp.shape
    xg = xp.reshape(B, S, POOL_GROUPS, POOL_GROUP_DIM).astype(jnp.float32)
    cs = jnp.concatenate([jnp.zeros((B, 1, POOL_GROUPS, POOL_GROUP_DIM), jnp.float32),
                          jnp.cumsum(xg, axis=1)], axis=1)
    t = jnp.arange(S)
    outs = []
    for g, w in enumerate(POOL_WINDOWS):
        lo = jnp.maximum(t + 1 - w, 0)
        cnt = jnp.minimum(t + 1, w).astype(jnp.float32)[None, :, None]
        mean = (cs[:, 1:, g] - cs[:, lo, g]) / cnt
        outs.append(mean - xg[:, :, g])
    y = jnp.stack(outs, axis=2).astype(xp.dtype)
    y = jnp.einsum('bsgc,gcd->bsgd', y, pool_w)
    return y.reshape(B, S, POOL_WIDTH) * pool_scale


def moba_attention(q, k, v):
    B, H, S, hd = q.shape
    L = MOBA_BLOCK
    nb = -(-S // L)
    pad = nb * L - S
    kp = jnp.pad(k, ((0, 0), (0, 0), (0, pad), (0, 0)))
    vp = jnp.pad(v, ((0, 0), (0, 0), (0, pad), (0, 0)))
    kb = kp.reshape(B, H, nb, L, hd)
    vb = vp.reshape(B, H, nb, L, hd)
    kmean = jnp.mean(kb.astype(jnp.float32), axis=3)
    topk = min(MOBA_TOPK, nb)
    scale = hd ** -0.5
    bi = jnp.arange(B)[:, None, None, None]
    hi = jnp.arange(H)[None, :, None, None]
    neg = jnp.float32(-1e30)

    def chunk(c):
        start = c * Q_CHUNK
        qc = lax.dynamic_slice_in_dim(q, start, Q_CHUNK, axis=2)
        qpos = start + jnp.arange(Q_CHUNK)
        own = start // L
        gate = jnp.einsum('bhqd,bhnd->bhqn', qc.astype(jnp.float32), kmean)
        gate = jnp.where(jnp.arange(nb) < own, gate, neg)
        _, gidx = lax.top_k(gate, topk)
        gvalid = gidx < own
        ksel = kb[bi, hi, gidx]
        vsel = vb[bi, hi, gidx]
        s_sel = jnp.einsum('bhqd,bhqnld->bhqnl', qc, ksel).astype(jnp.float32) * scale
        s_sel = jnp.where(gvalid[..., None], s_sel, neg)
        kown = lax.dynamic_index_in_dim(kb, own, axis=2, keepdims=False)
        vown = lax.dynamic_index_in_dim(vb, own, axis=2, keepdims=False)
        s_own = jnp.einsum('bhqd,bhld->bhql', qc, kown).astype(jnp.float32) * scale
        kpos = own * L + jnp.arange(L)
        s_own = jnp.where(kpos[None, :] <= qpos[:, None], s_own, neg)
        s = jnp.concatenate([s_sel.reshape(B, H, Q_CHUNK, topk * L), s_own], axis=-1)
        p = jax.nn.softmax(s, axis=-1).astype(v.dtype)
        p_sel = p[..., :topk * L].reshape(B, H, Q_CHUNK, topk, L)
        p_own = p[..., topk * L:]
        return (jnp.einsum('bhqnl,bhqnld->bhqd', p_sel, vsel)
                + jnp.einsum('bhql,bhld->bhqd', p_own, vown))

    outs = lax.map(chunk, jnp.arange(S // Q_CHUNK))
    return jnp.transpose(outs, (1, 2, 0, 3, 4)).reshape(B, H, S, hd)


def peer_ffn(xn, wq, subkeys, u, v):
    B, S, D = xn.shape
    q = (xn @ wq).reshape(B, S, PEER_HEADS, 2, PEER_HALF).astype(jnp.float32)
    s = jnp.einsum('bshpd,hpnd->bshpn', q, subkeys.astype(jnp.float32))
    sv, si = lax.top_k(s, PEER_TOPK)
    cand = sv[..., 0, :, None] + sv[..., 1, None, :]
    cv, ci = lax.top_k(cand.reshape(B, S, PEER_HEADS, PEER_TOPK * PEER_TOPK), PEER_TOPK)
    i1 = jnp.take_along_axis(si[..., 0, :], ci // PEER_TOPK, axis=-1)
    i2 = jnp.take_along_axis(si[..., 1, :], ci % PEER_TOPK, axis=-1)
    idx = i1 * PEER_NKEYS + i2
    g = jax.nn.softmax(cv, axis=-1).astype(xn.dtype)
    T = B * S
    HK = PEER_HEADS * PEER_TOPK
    n = T // PEER_CHUNK

    def body(args):
        xc, ic, gc = args
        a = jnp.einsum('cd,ced->ce', xc, u[ic])
        act = jax.nn.gelu(a, approximate=False) * gc
        return jnp.einsum('ce,ced->cd', act, v[ic])

    out = lax.map(body, (xn.reshape(n, PEER_CHUNK, D),
                         idx.reshape(n, PEER_CHUNK, HK),
                         g.reshape(n, PEER_CHUNK, HK)))
    return out.reshape(B, S, D)


def setup_inputs(seed: int = 0) -> dict:
    key = jax.random.key(seed)
    ks = jax.random.split(key, 16)
    f32 = jnp.float32
    nrm = lambda k, shape, s: jax.random.normal(k, shape, f32) * s
    return {
        "x": jax.random.normal(ks[0], (BATCH, SEQ, D_MODEL), f32),
        "norm_mix": 1.0 + nrm(ks[1], (DEPTH, D_MODEL), 0.02),
        "w_in": nrm(ks[2], (DEPTH, D_MODEL, IN_WIDTH), D_MODEL ** -0.5),
        "pool_w": nrm(ks[3], (DEPTH, POOL_GROUPS, POOL_GROUP_DIM, POOL_GROUP_DIM), POOL_GROUP_DIM ** -0.5),
        "pool_scale": 1.0 + nrm(ks[4], (DEPTH, POOL_WIDTH), 0.1),
        "q_norm": 1.0 + nrm(ks[5], (DEPTH, HEAD_DIM), 0.02),
        "k_norm": 1.0 + nrm(ks[6], (DEPTH, HEAD_DIM), 0.02),
        "w_pool_out": nrm(ks[7], (DEPTH, POOL_WIDTH, D_MODEL), POOL_WIDTH ** -0.5),
        "w_attn_out": nrm(ks[8], (DEPTH, ATTN_WIDTH, D_MODEL), ATTN_WIDTH ** -0.5),
        "w_o": nrm(ks[9], (DEPTH, D_MODEL, D_MODEL), D_MODEL ** -0.5),
        "norm_ffn": 1.0 + nrm(ks[10], (DEPTH, D_MODEL), 0.02),
        "peer_wq": nrm(ks[11], (DEPTH, D_MODEL, PEER_HEADS * PEER_QDIM), D_MODEL ** -0.5),
        "peer_subkeys": nrm(ks[12], (DEPTH, PEER_HEADS, 2, PEER_NKEYS, PEER_HALF), PEER_HALF ** -0.5),
        "peer_u": nrm(ks[13], (DEPTH, PEER_EXPERTS, D_MODEL), D_MODEL ** -0.5),
        "peer_v": nrm(ks[14], (DEPTH, PEER_EXPERTS, D_MODEL), 0.5),
    }


def reference(x, norm_mix, w_in, pool_w, pool_scale, q_norm, k_norm, w_pool_out,
              w_attn_out, w_o, norm_ffn, peer_wq, peer_subkeys, peer_u, peer_v):
    B, S, D = x.shape
    pos = jnp.arange(S)
    h = x
    for l in range(DEPTH):
        xn = rmsnorm(h, norm_mix[l])
        z = xn @ w_in[l]
        o = 0
        zp = z[..., o:o + POOL_WIDTH]; o += POOL_WIDTH
        zq = z[..., o:o + ATTN_WIDTH]; o += ATTN_WIDTH
        zk = z[..., o:o + ATTN_WIDTH]; o += ATTN_WIDTH
        zv = z[..., o:o + ATTN_WIDTH]; o += ATTN_WIDTH
        g_pool = z[..., o:o + D]; o += D
        g_attn = z[..., o:o + D]

        y_pool = pool_mixer(zp, pool_w[l], pool_scale[l])

        heads = lambda t: jnp.transpose(t.reshape(B, S, ATTN_HEADS, HEAD_DIM), (0, 2, 1, 3))
        q = rope(rmsnorm(heads(zq), q_norm[l]), pos)
        k = rope(rmsnorm(heads(zk), k_norm[l]), pos)
        v = heads(zv)
        att = moba_attention(q, k, v)
        y_attn = jnp.transpose(att, (0, 2, 1, 3)).reshape(B, S, ATTN_WIDTH)

        merged = (jax.nn.sigmoid(g_pool) * (y_pool @ w_pool_out[l])
                  + jax.nn.sigmoid(g_attn) * (y_attn @ w_attn_out[l]))
        h = h + merged @ w_o[l]

        hn = rmsnorm(h, norm_ffn[l])
        h = h + peer_ffn(hn, peer_wq[l], peer_subkeys[l], peer_u[l], peer_v[l])
    return h
```

```python
import functools

import jax
import jax.numpy as jnp
from jax import lax
from jax.experimental import pallas as pl
from jax.experimental.pallas import tpu as pltpu

POOL_WINDOWS = (2, 4, 8, 16)
POOL_GROUPS = len(POOL_WINDOWS)
POOL_HALO = 16
HEAD_DIM = 128
MOBA_BLOCK = 256
MOBA_TOPK = 3
ROPE_THETA = 10000.0
PEER_HEADS = 8
PEER_NKEYS = 128
PEER_HALF = 128
PEER_TOPK = 16
EPS = 1e-6
NEG = -1e30

LANES = 128
VMEM_LIMIT_BYTES = 56 * 1024 * 1024

F32 = jnp.float32
BF16 = jnp.bfloat16


def _cparams(*sem):
    return pltpu.CompilerParams(dimension_semantics=sem, vmem_limit_bytes=VMEM_LIMIT_BYTES)


def _tile(n, want):
    t = min(n, want)
    assert n % t == 0, (n, t)
    return t


def _rms_kernel(x_ref, g_ref, o_ref, *, transpose):
    x = x_ref[...]
    r = lax.rsqrt(jnp.mean(x * x, axis=-1, keepdims=True) + EPS)
    y = x * r * g_ref[...]
    if transpose:
        y = y.T
    o_ref[...] = y.astype(o_ref.dtype)


def _rmsnorm_bf16(x2d, g, *, transpose, tm=256):
    T, D = x2d.shape
    tm = _tile(T, tm)
    if transpose:
        out_shape = jax.ShapeDtypeStruct((D, T), BF16)
        out_spec = pl.BlockSpec((D, tm), lambda i: (0, i))
    else:
        out_shape = jax.ShapeDtypeStruct((T, D), BF16)
        out_spec = pl.BlockSpec((tm, D), lambda i: (i, 0))
    return pl.pallas_call(
        functools.partial(_rms_kernel, transpose=transpose),
        out_shape=out_shape,
        grid=(T // tm,),
        in_specs=[pl.BlockSpec((tm, D), lambda i: (i, 0)),
                  pl.BlockSpec((1, D), lambda i: (0, 0))],
        out_specs=out_spec,
        compiler_params=_cparams("parallel"),
        name="rmsnorm_cast",
    )(x2d, g.reshape(1, D))


def _mm_kernel(a_ref, b_ref, o_ref):
    o_ref[...] = jnp.dot(a_ref[...], b_ref[...], preferred_element_type=F32).astype(o_ref.dtype)


def _matmul(a, b, *, tm=1024, tn=1024, out_dtype=F32, name="matmul"):
    M, K = a.shape
    _, N = b.shape
    tm, tn = _tile(M, tm), _tile(N, tn)
    return pl.pallas_call(
        _mm_kernel,
        out_shape=jax.ShapeDtypeStruct((M, N), out_dtype),
        grid=(M // tm, N // tn),
        in_specs=[pl.BlockSpec((tm, K), lambda i, j: (i, 0)),
                  pl.BlockSpec((K, tn), lambda i, j: (0, j))],
        out_specs=pl.BlockSpec((tm, tn), lambda i, j: (i, j)),
        compiler_params=_cparams("parallel", "arbitrary"),
        name=name,
    )(a, b)


def _pool_kernel(x_ref, halo_ref, pw_ref, ps_ref, o_ref, *, ts, C):
    i = pl.program_id(1)
    pos = i * ts + lax.broadcasted_iota(jnp.int32, (ts, 1), 0)
    for g, w in enumerate(POOL_WINDOWS):
        x = x_ref[0, :, g * C:(g + 1) * C]
        halo = halo_ref[0, :, g * C:(g + 1) * C]
        halo = jnp.where(i == 0, 0.0, halo)
        s = jnp.concatenate([halo, x], axis=0)
        d = 1
        while d < w:
            s = s + pltpu.roll(s, d, axis=0)
            d *= 2
        s = s[POOL_HALO:]
        cnt = jnp.minimum(pos + 1, w).astype(F32)
        y = (s / cnt - x).astype(BF16)
        o = jnp.dot(y, pw_ref[g], preferred_element_type=F32) * ps_ref[:, g * C:(g + 1) * C]
        o_ref[:, g * C:(g + 1) * C] = o.astype(o_ref.dtype)


def _pool_mixer(z3, pool_w_bf, pool_scale, *, ts=512):
    B, S, _ = z3.shape
    G, C, _ = pool_w_bf.shape
    W = G * C
    ts = _tile(S, ts)
    hb = ts // POOL_HALO
    n_s = S // ts
    return pl.pallas_call(
        functools.partial(_pool_kernel, ts=ts, C=C),
        out_shape=jax.ShapeDtypeStruct((B * S, W), BF16),
        grid=(B, S // ts),
        in_specs=[pl.BlockSpec((1, ts, W), lambda b, i: (b, i, 0)),
                  pl.BlockSpec((1, POOL_HALO, W), lambda b, i: (b, jnp.maximum(i * hb - 1, 0), 0)),
                  pl.BlockSpec((G, C, C), lambda b, i: (0, 0, 0)),
                  pl.BlockSpec((1, W), lambda b, i: (0, 0))],
        out_specs=pl.BlockSpec((ts, W), lambda b, i: (b * n_s + i, 0)),
        compiler_params=_cparams("parallel", "parallel"),
        name="pool_mixer",
    )(z3, z3, pool_w_bf, pool_scale.reshape(1, W))


def _norm_rope(t, gain, cos, sin_signed):
    r = lax.rsqrt(jnp.mean(t * t, axis=-1, keepdims=True) + EPS)
    t = t * r * gain
    return t * cos + pltpu.roll(t, HEAD_DIM // 2, axis=1) * sin_signed


def _moba_kernel(zq_ref, zk_ref, zv_ref, cos_ref, sin_ref, qg_ref, kg_ref, o_ref,
                 kr_ref, vt_ref, km_ref, sel_ref, *, nb):
    L = MOBA_BLOCK
    i = pl.program_id(1)
    scale = HEAD_DIM ** -0.5

    @pl.when(i == 0)
    def _prepare_keys_values():
        for j in range(nb):
            rows = slice(j * L, (j + 1) * L)
            k = _norm_rope(zk_ref[0, rows, :], kg_ref[...], cos_ref[rows, :], sin_ref[rows, :])
            kr_ref[j] = k.astype(BF16)
            km_ref[j:j + 1, :] = jnp.mean(k, axis=0, keepdims=True)
            vt_ref[j] = zv_ref[0, rows, :].T.astype(BF16)

    q0 = pl.multiple_of(i * L, L)
    q = _norm_rope(zq_ref[0], qg_ref[...], cos_ref[pl.ds(q0, L), :], sin_ref[pl.ds(q0, L), :])
    qt = q.T
    qtb = qt.astype(BF16)

    gate = jnp.dot(km_ref[...], qt, preferred_element_type=F32,
                   precision=lax.Precision.HIGHEST)
    blk = lax.broadcasted_iota(jnp.int32, (nb, L), 0)
    gate = jnp.where(blk < i, gate, NEG)
    sel = jnp.zeros((nb, L), F32)
    for _ in range(min(MOBA_TOPK, nb)):
        m = jnp.max(gate, axis=0, keepdims=True)
        idx = jnp.min(jnp.where(gate == m, blk, nb), axis=0, keepdims=True)
        hit = blk == idx
        sel = jnp.where(hit & (idx < i), 1.0, sel)
        gate = jnp.where(hit, -jnp.inf, gate)
    sel_ref[...] = sel

    def scores(j):
        return jnp.dot(kr_ref[j], qtb, preferred_element_type=F32) * scale

    kpos = lax.broadcasted_iota(jnp.int32, (L, L), 0)
    qpos = lax.broadcasted_iota(jnp.int32, (L, L), 1)
    s = jnp.where(kpos <= qpos, scores(i), NEG)
    m0 = jnp.max(s, axis=0, keepdims=True)
    p = jnp.exp(s - m0)
    l0 = jnp.sum(p, axis=0, keepdims=True)
    acc0 = jnp.dot(vt_ref[i], p.astype(BF16), preferred_element_type=F32)

    def body(j, carry):
        m, l, acc = carry
        picked = sel_ref[pl.ds(j, 1), :] > 0.0
        s = jnp.where(picked, scores(j), NEG)
        m_new = jnp.maximum(m, jnp.max(s, axis=0, keepdims=True))
        alpha = jnp.exp(m - m_new)
        p = jnp.exp(s - m_new)
        l = alpha * l + jnp.sum(p, axis=0, keepdims=True)
        acc = alpha * acc + jnp.dot(vt_ref[j], p.astype(BF16), preferred_element_type=F32)
        return m_new, l, acc

    _, l, acc = lax.fori_loop(0, i, body, (m0, l0, acc0))
    o_ref[0] = (acc / l).T.astype(o_ref.dtype)


def _moba_attention(z3, cos, sin_signed, q_gain, k_gain, *, n_heads, q_col, k_col, v_col):
    B, S, _ = z3.shape
    L = MOBA_BLOCK
    nb = S // L
    hd = HEAD_DIM
    return pl.pallas_call(
        functools.partial(_moba_kernel, nb=nb),
        out_shape=jax.ShapeDtypeStruct((B, S, n_heads * hd), BF16),
        grid=(B * n_heads, nb),
        in_specs=[pl.BlockSpec((1, L, hd), lambda bh, i: (bh // n_heads, i, q_col + bh % n_heads)),
                  pl.BlockSpec((1, S, hd), lambda bh, i: (bh // n_heads, 0, k_col + bh % n_heads)),
                  pl.BlockSpec((1, S, hd), lambda bh, i: (bh // n_heads, 0, v_col + bh % n_heads)),
                  pl.BlockSpec((S, hd), lambda bh, i: (0, 0)),
                  pl.BlockSpec((S, hd), lambda bh, i: (0, 0)),
                  pl.BlockSpec((1, hd), lambda bh, i: (0, 0)),
                  pl.BlockSpec((1, hd), lambda bh, i: (0, 0))],
        out_specs=pl.BlockSpec((1, L, hd), lambda bh, i: (bh // n_heads, i, bh % n_heads)),
        scratch_shapes=[pltpu.VMEM((nb, L, hd), BF16),
                        pltpu.VMEM((nb, hd, L), BF16),
                        pltpu.VMEM((nb, hd), F32),
                        pltpu.VMEM((nb, L), F32)],
        compiler_params=_cparams("parallel", "arbitrary"),
        name="moba_attention",
    )(z3, z3, z3, cos, sin_signed, q_gain.reshape(1, hd), k_gain.reshape(1, hd))


def _merge_kernel(yp_ref, ya_ref, wp_ref, wa_ref, gp_ref, ga_ref, o_ref):
    p = jnp.dot(yp_ref[...], wp_ref[...], preferred_element_type=F32)
    a = jnp.dot(ya_ref[...], wa_ref[...], preferred_element_type=F32)
    o_ref[...] = (jax.nn.sigmoid(gp_ref[...]) * p + jax.nn.sigmoid(ga_ref[...]) * a).astype(o_ref.dtype)


def _merge(y_pool, y_attn, wp, wa, z2, *, gp_col, ga_col, tm=512, tn=1024):
    T, K = y_pool.shape
    D = wp.shape[1]
    tm, tn = _tile(T, tm), _tile(D, tn)
    gp_blk, ga_blk = gp_col // tn, ga_col // tn
    return pl.pallas_call(
        _merge_kernel,
        out_shape=jax.ShapeDtypeStruct((T, D), BF16),
        grid=(T // tm, D // tn),
        in_specs=[pl.BlockSpec((tm, K), lambda i, j: (i, 0)),
                  pl.BlockSpec((tm, K), lambda i, j: (i, 0)),
                  pl.BlockSpec((K, tn), lambda i, j: (0, j)),
                  pl.BlockSpec((K, tn), lambda i, j: (0, j)),
                  pl.BlockSpec((tm, tn), lambda i, j: (i, gp_blk + j)),
                  pl.BlockSpec((tm, tn), lambda i, j: (i, ga_blk + j))],
        out_specs=pl.BlockSpec((tm, tn), lambda i, j: (i, j)),
        compiler_params=_cparams("parallel", "arbitrary"),
        name="gated_merge",
    )(y_pool, y_attn, wp, wa, z2, z2)


def _proj_res_kernel(m_ref, w_ref, x_ref, o_ref):
    o_ref[...] = x_ref[...] + jnp.dot(m_ref[...], w_ref[...], preferred_element_type=F32)


def _proj_residual(m, w, x2d, *, tm=1024, tn=1024):
    T, K = m.shape
    D = w.shape[1]
    tm, tn = _tile(T, tm), _tile(D, tn)
    return pl.pallas_call(
        _proj_res_kernel,
        out_shape=jax.ShapeDtypeStruct((T, D), F32),
        grid=(T // tm, D // tn),
        in_specs=[pl.BlockSpec((tm, K), lambda i, j: (i, 0)),
                  pl.BlockSpec((K, tn), lambda i, j: (0, j)),
                  pl.BlockSpec((tm, tn), lambda i, j: (i, j))],
        out_specs=pl.BlockSpec((tm, tn), lambda i, j: (i, j)),
        compiler_params=_cparams("parallel", "arbitrary"),
        name="out_proj_residual",
    )(m, w, x2d)


def _top16_rows(s, n):
    t = s.shape[1]
    rows = lax.broadcasted_iota(jnp.int32, (n, t), 0).astype(F32)
    krow = lax.broadcasted_iota(jnp.int32, (PEER_TOPK, t), 0)

    def body(k, carry):
        work, rank, sv = carry
        m = jnp.max(work, axis=0, keepdims=True)
        idx = jnp.min(jnp.where(work == m, rows, float(n)), axis=0, keepdims=True)
        hit = rows == idx
        work = jnp.where(hit, -jnp.inf, work)
        rank = jnp.where(hit, k.astype(F32), rank)
        sv = jnp.where(krow == k, m, sv)
        return work, rank, sv

    init = (s, jnp.full((n, t), float(PEER_TOPK), F32), jnp.zeros((PEER_TOPK, t), F32))
    _, rank, sv = lax.fori_loop(0, PEER_TOPK, body, init)
    return sv, rank


def _route_kernel(qt_ref, sub_ref, o_ref, lq_ref, cf_ref, r1_ref, e1_ref):
    K = PEER_TOPK
    t = qt_ref.shape[1]
    hi = lax.Precision.HIGHEST
    k2 = lax.broadcasted_iota(jnp.int32, (K, t), 0).astype(F32)
    for h in range(PEER_HEADS):
        base = h * 2 * PEER_HALF
        s0 = jnp.dot(sub_ref[h, 0], qt_ref[base:base + PEER_HALF, :],
                     preferred_element_type=F32, precision=hi)
        s1 = jnp.dot(sub_ref[h, 1], qt_ref[base + PEER_HALF:base + 2 * PEER_HALF, :],
                     preferred_element_type=F32, precision=hi)
        sv0, rank0 = _top16_rows(s0, PEER_NKEYS)
        sv1, rank1 = _top16_rows(s1, PEER_NKEYS)

        cand = tuple(sv0[k1:k1 + 1, :] + sv1 for k1 in range(K))
        cmax = sv0[0:1, :] + sv1[0:1, :]

        def body(_, carry):
            cand, cnt, z = carry
            m = cand[0]
            for c in cand[1:]:
                m = jnp.maximum(m, c)
            m = jnp.max(m, axis=0, keepdims=True)
            idx = None
            for k1, c in enumerate(cand):
                f = jnp.where(c == m, k2 + float(k1 * K), float(K * K))
                idx = f if idx is None else jnp.minimum(idx, f)
            idx = jnp.min(idx, axis=0, keepdims=True)
            cand = tuple(jnp.where(k2 + float(k1 * K) == idx, -jnp.inf, c) for k1, c in enumerate(cand))
            cnt = jnp.where(k2 == jnp.floor(idx * (1.0 / K)), cnt + 1.0, cnt)
            z = z + jnp.exp(m - cmax)
            return cand, cnt, z

        _, cnt, z = lax.fori_loop(0, K, body, (cand, jnp.zeros((K, t), F32), jnp.zeros((1, t), F32)))

        lq = jnp.zeros((PEER_NKEYS, t), F32)
        for k1 in range(K):
            lq = jnp.where(rank0 == float(k1), cnt[k1:k1 + 1, :], lq)
        lq_ref[h] = lq
        cf_ref[h] = jnp.exp(s0 - sv0[0:1, :]) / z
        e1_ref[h] = jnp.exp(s1 - sv1[0:1, :])
        r1_ref[h] = rank1

    def expert_rows(i1, _):
        acc = jnp.zeros((PEER_NKEYS, t), F32)
        for h in range(PEER_HEADS):
            lq = lq_ref[h, pl.ds(i1, 1), :]
            cf = cf_ref[h, pl.ds(i1, 1), :]
            acc = acc + jnp.where(r1_ref[h] < lq, cf * e1_ref[h], 0.0)
        o_ref[0, pl.ds(pl.multiple_of(i1 * PEER_NKEYS, PEER_NKEYS), PEER_NKEYS), :] = acc.astype(o_ref.dtype)
        return 0

    lax.fori_loop(0, PEER_NKEYS, expert_rows, 0)


def _peer_route(qt, subkeys):
    Q, T = qt.shape
    E = PEER_NKEYS * PEER_NKEYS
    t = LANES
    scratch = pltpu.VMEM((PEER_HEADS, PEER_NKEYS, t), F32)
    return pl.pallas_call(
        _route_kernel,
        out_shape=jax.ShapeDtypeStruct((T // t, E, t), BF16),
        grid=(T // t,),
        in_specs=[pl.BlockSpec((Q, t), lambda i: (0, i)),
                  pl.BlockSpec(subkeys.shape, lambda i: (0, 0, 0, 0))],
        out_specs=pl.BlockSpec((1, E, t), lambda i: (i, 0, 0)),
        scratch_shapes=[scratch, scratch, scratch, scratch],
        compiler_params=_cparams("parallel"),
        name="peer_route",
    )(qt, subkeys)


def _peer_kernel(hn_ref, u_ref, vt_ref, g_ref, o_ref):
    e = pl.program_id(1)

    @pl.when(e == 0)
    def _zero():
        o_ref[...] = jnp.zeros_like(o_ref)

    a = jnp.dot(u_ref[...], hn_ref[...], preferred_element_type=F32)
    cols = []
    for c in range(g_ref.shape[0]):
        ac = a[:, c * LANES:(c + 1) * LANES]
        gelu = 0.5 * ac * (1.0 + lax.erf(ac * (2.0 ** -0.5)))
        act = gelu * g_ref[c].astype(F32)
        cols.append(act.astype(BF16))
    act = jnp.concatenate(cols, axis=1) if len(cols) > 1 else cols[0]
    o_ref[...] += jnp.dot(vt_ref[...], act, preferred_element_type=F32)


def _peer_experts(hn_t, u_bf, vt_bf, g_tiles, *, tm=512, te=512):
    D, T = hn_t.shape
    E = u_bf.shape[0]
    tm, te = _tile(T, tm), _tile(E, te)
    return pl.pallas_call(
        _peer_kernel,
        out_shape=jax.ShapeDtypeStruct((D, T), F32),
        grid=(T // tm, E // te),
        in_specs=[pl.BlockSpec((D, tm), lambda i, e: (0, i)),
                  pl.BlockSpec((te, D), lambda i, e: (e, 0)),
                  pl.BlockSpec((D, te), lambda i, e: (0, e)),
                  pl.BlockSpec((tm // LANES, te, LANES), lambda i, e: (i, e, 0))],
        out_specs=pl.BlockSpec((D, tm), lambda i, e: (0, i)),
        compiler_params=_cparams("parallel", "arbitrary"),
        name="peer_experts",
    )(hn_t, u_bf, vt_bf, g_tiles)


def _add_t_kernel(h_ref, ot_ref, o_ref):
    o_ref[...] = h_ref[...] + ot_ref[...].T


def _add_transposed(h, out_t, *, tm=256):
    T, D = h.shape
    tm = _tile(T, tm)
    return pl.pallas_call(
        _add_t_kernel,
        out_shape=jax.ShapeDtypeStruct((T, D), F32),
        grid=(T // tm,),
        in_specs=[pl.BlockSpec((tm, D), lambda i: (i, 0)),
                  pl.BlockSpec((D, tm), lambda i: (0, i))],
        out_specs=pl.BlockSpec((tm, D), lambda i: (i, 0)),
        compiler_params=_cparams("parallel"),
        name="peer_residual",
    )(h, out_t)


def _rope_tables(S):
    half = HEAD_DIM // 2
    inv = ROPE_THETA ** (-jnp.arange(half, dtype=F32) / half)
    ang = jnp.arange(S).astype(F32)[:, None] * inv[None, :]
    cos, sin = jnp.cos(ang), jnp.sin(ang)
    return jnp.concatenate([cos, cos], axis=-1), jnp.concatenate([-sin, sin], axis=-1)


def kernel(x, norm_mix, w_in, pool_w, pool_scale, q_norm, k_norm, w_pool_out,
           w_attn_out, w_o, norm_ffn, peer_wq, peer_subkeys, peer_u, peer_v):
    B, S, D = x.shape
    T = B * S
    depth = w_in.shape[0]
    pool_width = POOL_GROUPS * pool_w.shape[-1]
    attn_width = w_attn_out.shape[1]
    n_heads = attn_width // HEAD_DIM
    in_width = w_in.shape[-1]
    assert in_width == pool_width + 3 * attn_width + 2 * D
    assert S % MOBA_BLOCK == 0 and T % LANES == 0
    q_col = pool_width // HEAD_DIM
    k_col = q_col + n_heads
    v_col = k_col + n_heads
    gp_col = pool_width + 3 * attn_width
    ga_col = gp_col + D
    cos, sin_signed = _rope_tables(S)

    h = x.reshape(T, D)
    for l in range(depth):
        xn = _rmsnorm_bf16(h, norm_mix[l], transpose=False)
        z = _matmul(xn, w_in[l].astype(BF16), name="in_proj")
        z3 = z.reshape(B, S, in_width)
        y_pool = _pool_mixer(z3, pool_w[l].astype(BF16), pool_scale[l])
        y_attn = _moba_attention(z3, cos, sin_signed, q_norm[l], k_norm[l], n_heads=n_heads,
                                 q_col=q_col, k_col=k_col, v_col=v_col).reshape(T, attn_width)
        merged = _merge(y_pool, y_attn, w_pool_out[l].astype(BF16), w_attn_out[l].astype(BF16), z,
                        gp_col=gp_col, ga_col=ga_col)
        h = _proj_residual(merged, w_o[l].astype(BF16), h)

        hn_t = _rmsnorm_bf16(h, norm_ffn[l], transpose=True)
        qt = _matmul(peer_wq[l].T.astype(BF16), hn_t, name="peer_query")
        g_tiles = _peer_route(qt, peer_subkeys[l])
        out_t = _peer_experts(hn_t, peer_u[l].astype(BF16), peer_v[l].T.astype(BF16), g_tiles)
        h = _add_transposed(h, out_t)
    return h.reshape(B, S, D)
```

```python
import functools

import jax
import jax.numpy as jnp
from jax import lax
from jax.experimental import pallas as pl
from jax.experimental.pallas import tpu as pltpu

POOL_WINDOWS = (2, 4, 8, 16)
POOL_GROUPS = len(POOL_WINDOWS)
POOL_HALO = 16
HEAD_DIM = 128
MOBA_BLOCK = 256
MOBA_TOPK = 3
MOBA_CHUNK = 4
MOBA_HEADS_PER_STEP = 2
ROPE_THETA = 10000.0
PEER_HEADS = 8
PEER_NKEYS = 128
PEER_HALF = 128
PEER_TOPK = 16
EPS = 1e-6
NEG = -1e30
LOG2_E = 1.4426950408889634

LANES = 128
VMEM_LIMIT_BYTES = 56 * 1024 * 1024

F32 = jnp.float32
BF16 = jnp.bfloat16


def _cparams(*sem):
    return pltpu.CompilerParams(dimension_semantics=sem, vmem_limit_bytes=VMEM_LIMIT_BYTES)


def _tile(n, want):
    t = min(n, want)
    assert n % t == 0, (n, t)
    return t


def _rms_kernel(x_ref, g_ref, o_ref, *, transpose):
    x = x_ref[...]
    r = lax.rsqrt(jnp.mean(x * x, axis=-1, keepdims=True) + EPS)
    y = x * r * g_ref[...]
    if transpose:
        y = y.T
    o_ref[...] = y.astype(o_ref.dtype)


def _rmsnorm_bf16(x2d, g, *, transpose, tm=256):
    T, D = x2d.shape
    tm = _tile(T, tm)
    if transpose:
        out_shape = jax.ShapeDtypeStruct((D, T), BF16)
        out_spec = pl.BlockSpec((D, tm), lambda i: (0, i))
    else:
        out_shape = jax.ShapeDtypeStruct((T, D), BF16)
        out_spec = pl.BlockSpec((tm, D), lambda i: (i, 0))
    return pl.pallas_call(
        functools.partial(_rms_kernel, transpose=transpose),
        out_shape=out_shape,
        grid=(T // tm,),
        in_specs=[pl.BlockSpec((tm, D), lambda i: (i, 0)),
                  pl.BlockSpec((1, D), lambda i: (0, 0))],
        out_specs=out_spec,
        compiler_params=_cparams("parallel"),
        name="rmsnorm_cast",
    )(x2d, g.reshape(1, D))


def _mm_kernel(a_ref, b_ref, o_ref):
    o_ref[...] = jnp.dot(a_ref[...], b_ref[...], preferred_element_type=F32).astype(o_ref.dtype)


def _matmul(a, b, *, tm=1024, tn=1024, out_dtype=F32, name="matmul"):
    M, K = a.shape
    _, N = b.shape
    tm, tn = _tile(M, tm), _tile(N, tn)
    return pl.pallas_call(
        _mm_kernel,
        out_shape=jax.ShapeDtypeStruct((M, N), out_dtype),
        grid=(M // tm, N // tn),
        in_specs=[pl.BlockSpec((tm, K), lambda i, j: (i, 0)),
                  pl.BlockSpec((K, tn), lambda i, j: (0, j))],
        out_specs=pl.BlockSpec((tm, tn), lambda i, j: (i, j)),
        compiler_params=_cparams("parallel", "arbitrary"),
        name=name,
    )(a, b)


def _pool_kernel(x_ref, halo_ref, pw_ref, ps_ref, o_ref, *, ts, C):
    i = pl.program_id(1)
    pos = i * ts + lax.broadcasted_iota(jnp.int32, (ts, 1), 0)
    for g, w in enumerate(POOL_WINDOWS):
        x = x_ref[0, :, g * C:(g + 1) * C]
        halo = halo_ref[0, :, g * C:(g + 1) * C]
        halo = jnp.where(i == 0, 0.0, halo)
        s = jnp.concatenate([halo, x], axis=0)
        d = 1
        while d < w:
            s = s + pltpu.roll(s, d, axis=0)
            d *= 2
        s = s[POOL_HALO:]
        cnt = jnp.minimum(pos + 1, w).astype(F32)
        y = (s / cnt - x).astype(BF16)
        o = jnp.dot(y, pw_ref[g], preferred_element_type=F32) * ps_ref[:, g * C:(g + 1) * C]
        o_ref[:, g * C:(g + 1) * C] = o.astype(o_ref.dtype)


def _pool_mixer(z3, pool_w_bf, pool_scale, *, ts=512):
    B, S, _ = z3.shape
    G, C, _ = pool_w_bf.shape
    W = G * C
    ts = _tile(S, ts)
    hb = ts // POOL_HALO
    n_s = S // ts
    return pl.pallas_call(
        functools.partial(_pool_kernel, ts=ts, C=C),
        out_shape=jax.ShapeDtypeStruct((B * S, W), BF16),
        grid=(B, S // ts),
        in_specs=[pl.BlockSpec((1, ts, W), lambda b, i: (b, i, 0)),
                  pl.BlockSpec((1, POOL_HALO, W), lambda b, i: (b, jnp.maximum(i * hb - 1, 0), 0)),
                  pl.BlockSpec((G, C, C), lambda b, i: (0, 0, 0)),
                  pl.BlockSpec((1, W), lambda b, i: (0, 0))],
        out_specs=pl.BlockSpec((ts, W), lambda b, i: (b * n_s + i, 0)),
        compiler_params=_cparams("parallel", "parallel"),
        name="pool_mixer",
    )(z3, z3, pool_w_bf, pool_scale.reshape(1, W))


def _norm_rope(t, gain, cos, sin_signed):
    r = lax.rsqrt(jnp.mean(t * t, axis=-1, keepdims=True) + EPS)
    t = t * r * gain
    return t * cos + pltpu.roll(t, HEAD_DIM // 2, axis=1) * sin_signed


def _moba_kernel(zq_ref, zk_ref, zv_ref, cos_ref, sin_ref, qg_ref, kg_ref, o_ref,
                 kr_ref, vt_ref, km_ref, bias_ref, *, nb, ch, hps):
    L = MOBA_BLOCK
    hd = HEAD_DIM
    i = pl.program_id(1)
    heads = [slice(hh * hd, (hh + 1) * hd) for hh in range(hps)]

    @pl.when(i == 0)
    def _prepare_keys_values():
        for hh, hs in enumerate(heads):
            for j in range(nb):
                rows = slice(j * L, (j + 1) * L)
                part = slice((j % ch) * L, (j % ch + 1) * L)
                k = _norm_rope(zk_ref[0, rows, hs], kg_ref[...], cos_ref[rows, :], sin_ref[rows, :])
                kr_ref[hh, j // ch, part, :] = k.astype(BF16)
                km_ref[hh, j:j + 1, :] = jnp.mean(k, axis=0, keepdims=True)
                vt_ref[hh, j // ch, :, part] = zv_ref[0, rows, hs].T.astype(BF16)

    q0 = pl.multiple_of(i * L, L)
    cos_q, sin_q = cos_ref[pl.ds(q0, L), :], sin_ref[pl.ds(q0, L), :]
    blk = lax.broadcasted_iota(jnp.int32, (nb, L), 0)
    qsb = []
    for hh, hs in enumerate(heads):
        qt = _norm_rope(zq_ref[0, :, hs], qg_ref[...], cos_q, sin_q).T
        qsb.append((qt * (hd ** -0.5 * LOG2_E)).astype(BF16))
        gate = jnp.dot(km_ref[hh], qt, preferred_element_type=F32,
                       precision=lax.Precision.HIGHEST)
        gate = jnp.where(blk < i, gate, NEG)
        bias = jnp.full((nb, L), NEG, F32)
        for _ in range(min(MOBA_TOPK, nb)):
            m = jnp.max(gate, axis=0, keepdims=True)
            idx = jnp.min(jnp.where(gate == m, blk, nb), axis=0, keepdims=True)
            hit = blk == idx
            bias = jnp.where(hit & (idx < i), 0.0, bias)
            gate = jnp.where(hit, -jnp.inf, gate)
        bias_ref[hh] = bias

    kpos = lax.broadcasted_iota(jnp.int32, (L, L), 0)
    qpos = lax.broadcasted_iota(jnp.int32, (L, L), 1)
    causal_bias = jnp.where(kpos <= qpos, 0.0, NEG)

    def update_head(hh, c, carry):
        m, l, acc = carry
        s = jnp.dot(kr_ref[hh, c], qsb[hh], preferred_element_type=F32)
        parts = []
        for b in range(ch):
            j = c * ch + b
            bj = jnp.where(j == i, causal_bias, bias_ref[hh, pl.ds(j, 1), :])
            parts.append(s[b * L:(b + 1) * L] + bj)
        mx = parts[0]
        for sb in parts[1:]:
            mx = jnp.maximum(mx, sb)
        m_new = jnp.maximum(m, jnp.max(mx, axis=0, keepdims=True))
        alpha = jnp.exp2(m - m_new)
        psum, pv = None, None
        for b, sb in enumerate(parts):
            p = jnp.exp2(sb - m_new)
            d = jnp.dot(vt_ref[hh, c, :, b * L:(b + 1) * L], p.astype(BF16), preferred_element_type=F32)
            psum = p if psum is None else psum + p
            pv = d if pv is None else pv + d
        return m_new, alpha * l + jnp.sum(psum, axis=0, keepdims=True), alpha * acc + pv

    def update(c, carry):
        return tuple(update_head(hh, c, carry[hh]) for hh in range(hps))

    init = tuple((jnp.full((1, L), NEG, F32), jnp.zeros((1, L), F32), jnp.zeros((hd, L), F32))
                 for _ in range(hps))
    res = lax.fori_loop(0, i // ch + 1, update, init)
    for hs, (_, l, acc) in zip(heads, res):
        o_ref[0, :, hs] = (acc / l).T.astype(o_ref.dtype)


def _moba_attention(z3, cos, sin_signed, q_gain, k_gain, *, n_heads, q_col, k_col, v_col,
                    chunk=MOBA_CHUNK, hps=MOBA_HEADS_PER_STEP):
    B, S, _ = z3.shape
    L = MOBA_BLOCK
    nb = S // L
    hd = HEAD_DIM
    ch = chunk
    while nb % ch:
        ch //= 2
    assert n_heads % hps == 0 and q_col % hps == 0 and k_col % hps == 0 and v_col % hps == 0
    g = n_heads // hps
    w = hps * hd
    return pl.pallas_call(
        functools.partial(_moba_kernel, nb=nb, ch=ch, hps=hps),
        out_shape=jax.ShapeDtypeStruct((B, S, n_heads * hd), BF16),
        grid=(B * g, nb),
        in_specs=[pl.BlockSpec((1, L, w), lambda bg, i: (bg // g, i, q_col // hps + bg % g)),
                  pl.BlockSpec((1, S, w), lambda bg, i: (bg // g, 0, k_col // hps + bg % g)),
                  pl.BlockSpec((1, S, w), lambda bg, i: (bg // g, 0, v_col // hps + bg % g)),
                  pl.BlockSpec((S, hd), lambda bg, i: (0, 0)),
                  pl.BlockSpec((S, hd), lambda bg, i: (0, 0)),
                  pl.BlockSpec((1, hd), lambda bg, i: (0, 0)),
                  pl.BlockSpec((1, hd), lambda bg, i: (0, 0))],
        out_specs=pl.BlockSpec((1, L, w), lambda bg, i: (bg // g, i, bg % g)),
        scratch_shapes=[pltpu.VMEM((hps, nb // ch, ch * L, hd), BF16),
                        pltpu.VMEM((hps, nb // ch, hd, ch * L), BF16),
                        pltpu.VMEM((hps, nb, hd), F32),
                        pltpu.VMEM((hps, nb, L), F32)],
        compiler_params=_cparams("parallel", "arbitrary"),
        name="moba_attention",
    )(z3, z3, z3, cos, sin_signed, q_gain.reshape(1, hd), k_gain.reshape(1, hd))


def _merge_kernel(yp_ref, ya_ref, wp_ref, wa_ref, gp_ref, ga_ref, o_ref):
    p = jnp.dot(yp_ref[...], wp_ref[...], preferred_element_type=F32)
    a = jnp.dot(ya_ref[...], wa_ref[...], preferred_element_type=F32)
    o_ref[...] = (jax.nn.sigmoid(gp_ref[...]) * p + jax.nn.sigmoid(ga_ref[...]) * a).astype(o_ref.dtype)


def _merge(y_pool, y_attn, wp, wa, z2, *, gp_col, ga_col, tm=512, tn=1024):
    T, K = y_pool.shape
    D = wp.shape[1]
    tm, tn = _tile(T, tm), _tile(D, tn)
    gp_blk, ga_blk = gp_col // tn, ga_col // tn
    return pl.pallas_call(
        _merge_kernel,
        out_shape=jax.ShapeDtypeStruct((T, D), BF16),
        grid=(T // tm, D // tn),
        in_specs=[pl.BlockSpec((tm, K), lambda i, j: (i, 0)),
                  pl.BlockSpec((tm, K), lambda i, j: (i, 0)),
                  pl.BlockSpec((K, tn), lambda i, j: (0, j)),
                  pl.BlockSpec((K, tn), lambda i, j: (0, j)),
                  pl.BlockSpec((tm, tn), lambda i, j: (i, gp_blk + j)),
                  pl.BlockSpec((tm, tn), lambda i, j: (i, ga_blk + j))],
        out_specs=pl.BlockSpec((tm, tn), lambda i, j: (i, j)),
        compiler_params=_cparams("parallel", "arbitrary"),
        name="gated_merge",
    )(y_pool, y_attn, wp, wa, z2, z2)


def _proj_res_kernel(m_ref, w_ref, x_ref, o_ref):
    o_ref[...] = x_ref[...] + jnp.dot(m_ref[...], w_ref[...], preferred_element_type=F32)


def _proj_residual(m, w, x2d, *, tm=1024, tn=1024):
    T, K = m.shape
    D = w.shape[1]
    tm, tn = _tile(T, tm), _tile(D, tn)
    return pl.pallas_call(
        _proj_res_kernel,
        out_shape=jax.ShapeDtypeStruct((T, D), F32),
        grid=(T // tm, D // tn),
        in_specs=[pl.BlockSpec((tm, K), lambda i, j: (i, 0)),
                  pl.BlockSpec((K, tn), lambda i, j: (0, j)),
                  pl.BlockSpec((tm, tn), lambda i, j: (i, j))],
        out_specs=pl.BlockSpec((tm, tn), lambda i, j: (i, j)),
        compiler_params=_cparams("parallel", "arbitrary"),
        name="out_proj_residual",
    )(m, w, x2d)


def _top16_rows(scores):
    n, t = scores[0].shape
    rows = lax.broadcasted_iota(jnp.int32, (n, t), 0).astype(F32)
    krow = lax.broadcasted_iota(jnp.int32, (PEER_TOPK, t), 0)

    def body(k, carry):
        out = []
        for work, rank, sv in carry:
            m = jnp.max(work, axis=0, keepdims=True)
            idx = jnp.min(jnp.where(work == m, rows, float(n)), axis=0, keepdims=True)
            hit = rows == idx
            out.append((jnp.where(hit, -jnp.inf, work),
                        jnp.where(hit, k.astype(F32), rank),
                        jnp.where(krow == k, m, sv)))
        return tuple(out)

    init = tuple((s, jnp.full((n, t), float(PEER_TOPK), F32), jnp.zeros((PEER_TOPK, t), F32))
                 for s in scores)
    res = lax.fori_loop(0, PEER_TOPK, body, init)
    return [(sv, rank) for _, rank, sv in res]


def _top16_pairs(sorted_pairs):
    K = PEER_TOPK
    t = sorted_pairs[0][0].shape[1]
    k2 = lax.broadcasted_iota(jnp.int32, (K, t), 0).astype(F32)
    flat = (k2,) + tuple(k2[:K // 2] + float(k1 * K) for k1 in range(1, K))
    cmax = tuple(sv0[0:1, :] + sv1[0:1, :] for sv0, sv1 in sorted_pairs)

    def body(_, carry):
        out = []
        for (cand, cnt, z), cm in zip(carry, cmax):
            m = jnp.max(cand[0], axis=0, keepdims=True)
            for c in cand[1:]:
                m = jnp.maximum(m, jnp.max(c, axis=0, keepdims=True))
            idx = None
            for c, f in zip(cand, flat):
                fi = jnp.min(jnp.where(c == m, f, float(K * K)), axis=0, keepdims=True)
                idx = fi if idx is None else jnp.minimum(idx, fi)
            cand = tuple(jnp.where(f == idx, -jnp.inf, c) for c, f in zip(cand, flat))
            cnt = jnp.where(k2 == jnp.floor(idx * (1.0 / K)), cnt + 1.0, cnt)
            out.append((cand, cnt, z + jnp.exp(m - cm)))
        return tuple(out)

    init = tuple(((sv0[0:1, :] + sv1,) + tuple(sv0[k1:k1 + 1, :] + sv1[:K // 2] for k1 in range(1, K)),
                  jnp.zeros((K, t), F32), jnp.zeros((1, t), F32)) for sv0, sv1 in sorted_pairs)
    res = lax.fori_loop(0, K, body, init)
    return [(cnt, z) for _, cnt, z in res]


def _route_kernel(qt_ref, sub_ref, g_ref, thr_ref, cf_ref, k1_ref, e1_ref):
    hi = lax.Precision.HIGHEST
    t = qt_ref.shape[1]
    lane_tiles = [slice(c * LANES, (c + 1) * LANES) for c in range(t // LANES)]
    for h in range(PEER_HEADS):
        base = h * 2 * PEER_HALF
        scores = []
        for lt in lane_tiles:
            for half in range(2):
                rows = slice(base + half * PEER_HALF, base + (half + 1) * PEER_HALF)
                scores.append(jnp.dot(sub_ref[h, half], qt_ref[rows, lt],
                                      preferred_element_type=F32, precision=hi))
        tops = _top16_rows(scores)
        stairs = _top16_pairs([(tops[2 * c][0], tops[2 * c + 1][0]) for c in range(len(lane_tiles))])
        for c, lt in enumerate(lane_tiles):
            s0, s1 = scores[2 * c], scores[2 * c + 1]
            (sv0, rank0), (sv1, rank1) = tops[2 * c], tops[2 * c + 1]
            cnt, z = stairs[c]
            paired = jnp.zeros_like(s0)
            for k1 in range(PEER_TOPK):
                paired = jnp.where(rank0 == float(k1), cnt[k1:k1 + 1, :], paired)
            thr_ref[h, :, lt] = 1.0 - paired
            cf_ref[h, :, lt] = jnp.exp(s0 - sv0[0:1, :]) / z
            e1_ref[h, :, lt] = jnp.exp(s1 - sv1[0:1, :])
            k1_ref[h, :, lt] = -rank1

    def expert_rows(i1, _):
        gate = None
        for h in range(PEER_HEADS):
            thr = thr_ref[h, pl.ds(i1, 1), :]
            cf = cf_ref[h, pl.ds(i1, 1), :]
            g = jnp.where(k1_ref[h] >= thr, cf * e1_ref[h], 0.0)
            gate = g if gate is None else gate + g
        rows = pl.ds(pl.multiple_of(i1 * PEER_NKEYS, PEER_NKEYS), PEER_NKEYS)
        for c, lt in enumerate(lane_tiles):
            g_ref[c, rows, :] = gate[:, lt].astype(g_ref.dtype)
        return 0

    lax.fori_loop(0, PEER_NKEYS, expert_rows, 0)


def _peer_route(qt, subkeys, *, t=256):
    Q, T = qt.shape
    t = _tile(T, t)
    E = PEER_NKEYS * PEER_NKEYS
    scratch = pltpu.VMEM((PEER_HEADS, PEER_NKEYS, t), F32)
    return pl.pallas_call(
        _route_kernel,
        out_shape=jax.ShapeDtypeStruct((T // LANES, E, LANES), BF16),
        grid=(T // t,),
        in_specs=[pl.BlockSpec((Q, t), lambda i: (0, i)),
                  pl.BlockSpec(subkeys.shape, lambda i: (0, 0, 0, 0))],
        out_specs=pl.BlockSpec((t // LANES, E, LANES), lambda i: (i, 0, 0)),
        scratch_shapes=[scratch, scratch, scratch, scratch],
        compiler_params=_cparams("parallel"),
        name="peer_route",
    )(qt, subkeys)


def _peer_kernel(hn_ref, u_ref, vt_ref, g_ref, o_ref):
    e = pl.program_id(1)

    @pl.when(e == 0)
    def _zero():
        o_ref[...] = jnp.zeros_like(o_ref)

    a = jnp.dot(u_ref[...], hn_ref[...], preferred_element_type=F32)
    cols = []
    for c in range(g_ref.shape[0]):
        ac = a[:, c * LANES:(c + 1) * LANES]
        gelu = 0.5 * ac * (1.0 + lax.erf(ac * (2.0 ** -0.5)))
        cols.append((gelu * g_ref[c].astype(F32)).astype(BF16))
    act = jnp.concatenate(cols, axis=1) if len(cols) > 1 else cols[0]
    o_ref[...] += jnp.dot(vt_ref[...], act, preferred_element_type=F32)


def _peer_experts(hn_t, u_bf, vt_bf, g_tiles, *, tm=512, te=512):
    D, T = hn_t.shape
    E = u_bf.shape[0]
    tm, te = _tile(T, tm), _tile(E, te)
    return pl.pallas_call(
        _peer_kernel,
        out_shape=jax.ShapeDtypeStruct((D, T), F32),
        grid=(T // tm, E // te),
        in_specs=[pl.BlockSpec((D, tm), lambda i, e: (0, i)),
                  pl.BlockSpec((te, D), lambda i, e: (e, 0)),
                  pl.BlockSpec((D, te), lambda i, e: (0, e)),
                  pl.BlockSpec((tm // LANES, te, LANES), lambda i, e: (i, e, 0))],
        out_specs=pl.BlockSpec((D, tm), lambda i, e: (0, i)),
        compiler_params=_cparams("parallel", "arbitrary"),
        name="peer_experts",
    )(hn_t, u_bf, vt_bf, g_tiles)


def _add_t_kernel(h_ref, ot_ref, o_ref):
    o_ref[...] = h_ref[...] + ot_ref[...].T


def _add_transposed(h, out_t, *, tm=256):
    T, D = h.shape
    tm = _tile(T, tm)
    return pl.pallas_call(
        _add_t_kernel,
        out_shape=jax.ShapeDtypeStruct((T, D), F32),
        grid=(T // tm,),
        in_specs=[pl.BlockSpec((tm, D), lambda i: (i, 0)),
                  pl.BlockSpec((D, tm), lambda i: (0, i))],
        out_specs=pl.BlockSpec((tm, D), lambda i: (i, 0)),
        compiler_params=_cparams("parallel"),
        name="peer_residual",
    )(h, out_t)


def _rope_tables(S):
    half = HEAD_DIM // 2
    inv = ROPE_THETA ** (-jnp.arange(half, dtype=F32) / half)
    ang = jnp.arange(S).astype(F32)[:, None] * inv[None, :]
    cos, sin = jnp.cos(ang), jnp.sin(ang)
    return jnp.concatenate([cos, cos], axis=-1), jnp.concatenate([-sin, sin], axis=-1)


def kernel(x, norm_mix, w_in, pool_w, pool_scale, q_norm, k_norm, w_pool_out,
           w_attn_out, w_o, norm_ffn, peer_wq, peer_subkeys, peer_u, peer_v):
    B, S, D = x.shape
    T = B * S
    depth = w_in.shape[0]
    pool_width = POOL_GROUPS * pool_w.shape[-1]
    attn_width = w_attn_out.shape[1]
    n_heads = attn_width // HEAD_DIM
    in_width = w_in.shape[-1]
    assert in_width == pool_width + 3 * attn_width + 2 * D
    assert S % MOBA_BLOCK == 0 and T % LANES == 0
    q_col = pool_width // HEAD_DIM
    k_col = q_col + n_heads
    v_col = k_col + n_heads
    gp_col = pool_width + 3 * attn_width
    ga_col = gp_col + D
    cos, sin_signed = _rope_tables(S)

    h = x.reshape(T, D)
    for l in range(depth):
        xn = _rmsnorm_bf16(h, norm_mix[l], transpose=False)
        z = _matmul(xn, w_in[l].astype(BF16), name="in_proj")
        z3 = z.reshape(B, S, in_width)
        y_pool = _pool_mixer(z3, pool_w[l].astype(BF16), pool_scale[l])
        y_attn = _moba_attention(z3, cos, sin_signed, q_norm[l], k_norm[l], n_heads=n_heads,
                                 q_col=q_col, k_col=k_col, v_col=v_col).reshape(T, attn_width)
        merged = _merge(y_pool, y_attn, w_pool_out[l].astype(BF16), w_attn_out[l].astype(BF16), z,
                        gp_col=gp_col, ga_col=ga_col)
        h = _proj_residual(merged, w_o[l].astype(BF16), h)

        hn_t = _rmsnorm_bf16(h, norm_ffn[l], transpose=True)
        qt = _matmul(peer_wq[l].T.astype(BF16), hn_t, name="peer_query")
        g_tiles = _peer_route(qt, peer_subkeys[l])
        out_t = _peer_experts(hn_t, peer_u[l].astype(BF16), peer_v[l].T.astype(BF16), g_tiles)
        h = _add_transposed(h, out_t)
    return h.reshape(B, S, D)
```

```python
import functools

import jax
import jax.numpy as jnp
from jax import lax
from jax.experimental import pallas as pl
from jax.experimental.pallas import tpu as pltpu

POOL_WINDOWS = (2, 4, 8, 16)
POOL_GROUPS = len(POOL_WINDOWS)
POOL_HALO = 16
HEAD_DIM = 128
MOBA_BLOCK = 256
MOBA_TOPK = 3
MOBA_CHUNK = 4
MOBA_HEADS_PER_STEP = 2
ROPE_THETA = 10000.0
PEER_HEADS = 8
PEER_NKEYS = 128
PEER_HALF = 128
PEER_TOPK = 16
EPS = 1e-6
NEG = -1e30
LOG2_E = 1.4426950408889634

LANES = 128
SUBLANES = 8
VMEM_LIMIT_BYTES = 56 * 1024 * 1024

F32 = jnp.float32
BF16 = jnp.bfloat16


def _cparams(*sem):
    return pltpu.CompilerParams(dimension_semantics=sem, vmem_limit_bytes=VMEM_LIMIT_BYTES)


def _tile(n, want):
    t = min(n, want)
    assert n % t == 0, (n, t)
    return t


def _rms_kernel(x_ref, g_ref, o_ref, *, transpose):
    x = x_ref[...]
    r = lax.rsqrt(jnp.mean(x * x, axis=-1, keepdims=True) + EPS)
    y = x * r * g_ref[...]
    if transpose:
        y = y.T
    o_ref[...] = y.astype(o_ref.dtype)


def _rmsnorm_bf16(x2d, g, *, transpose, tm=256):
    T, D = x2d.shape
    tm = _tile(T, tm)
    if transpose:
        out_shape = jax.ShapeDtypeStruct((D, T), BF16)
        out_spec = pl.BlockSpec((D, tm), lambda i: (0, i))
    else:
        out_shape = jax.ShapeDtypeStruct((T, D), BF16)
        out_spec = pl.BlockSpec((tm, D), lambda i: (i, 0))
    return pl.pallas_call(
        functools.partial(_rms_kernel, transpose=transpose),
        out_shape=out_shape,
        grid=(T // tm,),
        in_specs=[pl.BlockSpec((tm, D), lambda i: (i, 0)),
                  pl.BlockSpec((1, D), lambda i: (0, 0))],
        out_specs=out_spec,
        compiler_params=_cparams("parallel"),
        name="rmsnorm_cast",
    )(x2d, g.reshape(1, D))


def _mm_kernel(a_ref, b_ref, o_ref):
    o_ref[...] = jnp.dot(a_ref[...], b_ref[...], preferred_element_type=F32).astype(o_ref.dtype)


def _matmul(a, b, *, tm=1024, tn=1024, out_dtype=F32, name="matmul"):
    M, K = a.shape
    _, N = b.shape
    tm, tn = _tile(M, tm), _tile(N, tn)
    return pl.pallas_call(
        _mm_kernel,
        out_shape=jax.ShapeDtypeStruct((M, N), out_dtype),
        grid=(M // tm, N // tn),
        in_specs=[pl.BlockSpec((tm, K), lambda i, j: (i, 0)),
                  pl.BlockSpec((K, tn), lambda i, j: (0, j))],
        out_specs=pl.BlockSpec((tm, tn), lambda i, j: (i, j)),
        compiler_params=_cparams("parallel", "arbitrary"),
        name=name,
    )(a, b)


def _pool_kernel(x_ref, halo_ref, pw_ref, ps_ref, o_ref, *, ts, C):
    i = pl.program_id(1)
    pos = i * ts + lax.broadcasted_iota(jnp.int32, (ts, 1), 0)
    for g, w in enumerate(POOL_WINDOWS):
        x = x_ref[0, :, g * C:(g + 1) * C]
        halo = halo_ref[0, :, g * C:(g + 1) * C]
        halo = jnp.where(i == 0, 0.0, halo)
        s = jnp.concatenate([halo, x], axis=0)
        d = 1
        while d < w:
            s = s + pltpu.roll(s, d, axis=0)
            d *= 2
        s = s[POOL_HALO:]
        cnt = jnp.minimum(pos + 1, w).astype(F32)
        y = (s / cnt - x).astype(BF16)
        o = jnp.dot(y, pw_ref[g], preferred_element_type=F32) * ps_ref[:, g * C:(g + 1) * C]
        o_ref[:, g * C:(g + 1) * C] = o.astype(o_ref.dtype)


def _pool_mixer(z3, pool_w_bf, pool_scale, *, ts=512):
    B, S, _ = z3.shape
    G, C, _ = pool_w_bf.shape
    W = G * C
    ts = _tile(S, ts)
    hb = ts // POOL_HALO
    n_s = S // ts
    return pl.pallas_call(
        functools.partial(_pool_kernel, ts=ts, C=C),
        out_shape=jax.ShapeDtypeStruct((B * S, W), BF16),
        grid=(B, S // ts),
        in_specs=[pl.BlockSpec((1, ts, W), lambda b, i: (b, i, 0)),
                  pl.BlockSpec((1, POOL_HALO, W), lambda b, i: (b, jnp.maximum(i * hb - 1, 0), 0)),
                  pl.BlockSpec((G, C, C), lambda b, i: (0, 0, 0)),
                  pl.BlockSpec((1, W), lambda b, i: (0, 0))],
        out_specs=pl.BlockSpec((ts, W), lambda b, i: (b * n_s + i, 0)),
        compiler_params=_cparams("parallel", "parallel"),
        name="pool_mixer",
    )(z3, z3, pool_w_bf, pool_scale.reshape(1, W))


def _norm_rope(t, gain, cos, sin_signed):
    r = lax.rsqrt(jnp.mean(t * t, axis=-1, keepdims=True) + EPS)
    t = t * r * gain
    return t * cos + pltpu.roll(t, HEAD_DIM // 2, axis=1) * sin_signed


def _moba_kernel(zq_ref, zk_ref, zv_ref, cos_ref, sin_ref, qg_ref, kg_ref, o_ref,
                 kr_ref, vt_ref, km_ref, bias_ref, s_ref, *, nb, ch, hps):
    L = MOBA_BLOCK
    hd = HEAD_DIM
    i = pl.program_id(1)
    heads = [slice(hh * hd, (hh + 1) * hd) for hh in range(hps)]

    @pl.when(i == 0)
    def _prepare_keys_values():
        for hh, hs in enumerate(heads):
            for j in range(nb):
                rows = slice(j * L, (j + 1) * L)
                part = slice((j % ch) * L, (j % ch + 1) * L)
                k = _norm_rope(zk_ref[0, rows, hs], kg_ref[...], cos_ref[rows, :], sin_ref[rows, :])
                kr_ref[hh, j // ch, part, :] = k.astype(BF16)
                km_ref[hh, j:j + 1, :] = jnp.mean(k, axis=0, keepdims=True)
                vt_ref[hh, j // ch, :, part] = zv_ref[0, rows, hs].T.astype(BF16)

    q0 = pl.multiple_of(i * L, L)
    cos_q, sin_q = cos_ref[pl.ds(q0, L), :], sin_ref[pl.ds(q0, L), :]
    blk = lax.broadcasted_iota(jnp.int32, (nb, L), 0)
    qsb = []
    for hh, hs in enumerate(heads):
        qt = _norm_rope(zq_ref[0, :, hs], qg_ref[...], cos_q, sin_q).T
        qsb.append((qt * (hd ** -0.5 * LOG2_E)).astype(BF16))
        gate = jnp.dot(km_ref[hh], qt, preferred_element_type=F32,
                       precision=lax.Precision.HIGHEST)
        gate = jnp.where(blk < i, gate, NEG)
        bias = jnp.full((nb, L), NEG, F32)
        for _ in range(min(MOBA_TOPK, nb)):
            m = jnp.max(gate, axis=0, keepdims=True)
            idx = jnp.min(jnp.where(gate == m, blk, nb), axis=0, keepdims=True)
            hit = blk == idx
            bias = jnp.where(hit & (idx < i), 0.0, bias)
            gate = jnp.where(hit, -jnp.inf, gate)
        bias_ref[hh] = bias

    kpos = lax.broadcasted_iota(jnp.int32, (L, L), 0)
    qpos = lax.broadcasted_iota(jnp.int32, (L, L), 1)
    causal_bias = jnp.where(kpos <= qpos, 0.0, NEG)

    def fold(x, op):
        parts = [x[r * SUBLANES:(r + 1) * SUBLANES] for r in range(L // SUBLANES)]
        while len(parts) > 1:
            parts = [op(parts[k], parts[k + 1]) for k in range(0, len(parts), 2)]
        return parts[0]

    n_chunks = i // ch + 1

    def score_chunk(c, mx):
        out = []
        for hh in range(hps):
            s = jnp.dot(kr_ref[hh, c], qsb[hh], preferred_element_type=F32)
            m = mx[hh]
            for b in range(ch):
                j = c * ch + b
                bj = jnp.where(j == i, causal_bias, bias_ref[hh, pl.ds(j, 1), :])
                sb = s[b * L:(b + 1) * L] + bj
                s_ref[hh, pl.ds(pl.multiple_of(j * L, L), L), :] = sb
                m = jnp.maximum(m, fold(sb, jnp.maximum))
            out.append(m)
        return tuple(out)

    mx = lax.fori_loop(0, n_chunks, score_chunk,
                       tuple(jnp.full((SUBLANES, L), NEG, F32) for _ in range(hps)))
    mq = [jnp.max(m, axis=0, keepdims=True) for m in mx]

    def value_chunk(c, carry):
        out = []
        for hh in range(hps):
            ps, acc = carry[hh]
            for b in range(ch):
                j = c * ch + b
                p = jnp.exp2(s_ref[hh, pl.ds(pl.multiple_of(j * L, L), L), :] - mq[hh])
                ps = ps + fold(p, jnp.add)
                acc = acc + jnp.dot(vt_ref[hh, c, :, b * L:(b + 1) * L], p.astype(BF16),
                                    preferred_element_type=F32)
            out.append((ps, acc))
        return tuple(out)

    init = tuple((jnp.zeros((SUBLANES, L), F32), jnp.zeros((hd, L), F32)) for _ in range(hps))
    res = lax.fori_loop(0, n_chunks, value_chunk, init)
    for hs, (ps, acc) in zip(heads, res):
        o_ref[0, :, hs] = (acc / jnp.sum(ps, axis=0, keepdims=True)).T.astype(o_ref.dtype)


def _moba_attention(z3, cos, sin_signed, q_gain, k_gain, *, n_heads, q_col, k_col, v_col,
                    chunk=MOBA_CHUNK, hps=MOBA_HEADS_PER_STEP):
    B, S, _ = z3.shape
    L = MOBA_BLOCK
    nb = S // L
    hd = HEAD_DIM
    ch = chunk
    while nb % ch:
        ch //= 2
    assert n_heads % hps == 0 and q_col % hps == 0 and k_col % hps == 0 and v_col % hps == 0
    g = n_heads // hps
    w = hps * hd
    return pl.pallas_call(
        functools.partial(_moba_kernel, nb=nb, ch=ch, hps=hps),
        out_shape=jax.ShapeDtypeStruct((B, S, n_heads * hd), BF16),
        grid=(B * g, nb),
        in_specs=[pl.BlockSpec((1, L, w), lambda bg, i: (bg // g, i, q_col // hps + bg % g)),
                  pl.BlockSpec((1, S, w), lambda bg, i: (bg // g, 0, k_col // hps + bg % g)),
                  pl.BlockSpec((1, S, w), lambda bg, i: (bg // g, 0, v_col // hps + bg % g)),
                  pl.BlockSpec((S, hd), lambda bg, i: (0, 0)),
                  pl.BlockSpec((S, hd), lambda bg, i: (0, 0)),
                  pl.BlockSpec((1, hd), lambda bg, i: (0, 0)),
                  pl.BlockSpec((1, hd), lambda bg, i: (0, 0))],
        out_specs=pl.BlockSpec((1, L, w), lambda bg, i: (bg // g, i, bg % g)),
        scratch_shapes=[pltpu.VMEM((hps, nb // ch, ch * L, hd), BF16),
                        pltpu.VMEM((hps, nb // ch, hd, ch * L), BF16),
                        pltpu.VMEM((hps, nb, hd), F32),
                        pltpu.VMEM((hps, nb, L), F32),
                        pltpu.VMEM((hps, S, L), F32)],
        compiler_params=_cparams("parallel", "arbitrary"),
        name="moba_attention",
    )(z3, z3, z3, cos, sin_signed, q_gain.reshape(1, hd), k_gain.reshape(1, hd))


def _merge_kernel(yp_ref, ya_ref, wp_ref, wa_ref, gp_ref, ga_ref, o_ref):
    p = jnp.dot(yp_ref[...], wp_ref[...], preferred_element_type=F32)
    a = jnp.dot(ya_ref[...], wa_ref[...], preferred_element_type=F32)
    o_ref[...] = (jax.nn.sigmoid(gp_ref[...]) * p + jax.nn.sigmoid(ga_ref[...]) * a).astype(o_ref.dtype)


def _merge(y_pool, y_attn, wp, wa, z2, *, gp_col, ga_col, tm=512, tn=1024):
    T, K = y_pool.shape
    D = wp.shape[1]
    tm, tn = _tile(T, tm), _tile(D, tn)
    gp_blk, ga_blk = gp_col // tn, ga_col // tn
    return pl.pallas_call(
        _merge_kernel,
        out_shape=jax.ShapeDtypeStruct((T, D), BF16),
        grid=(T // tm, D // tn),
        in_specs=[pl.BlockSpec((tm, K), lambda i, j: (i, 0)),
                  pl.BlockSpec((tm, K), lambda i, j: (i, 0)),
                  pl.BlockSpec((K, tn), lambda i, j: (0, j)),
                  pl.BlockSpec((K, tn), lambda i, j: (0, j)),
                  pl.BlockSpec((tm, tn), lambda i, j: (i, gp_blk + j)),
                  pl.BlockSpec((tm, tn), lambda i, j: (i, ga_blk + j))],
        out_specs=pl.BlockSpec((tm, tn), lambda i, j: (i, j)),
        compiler_params=_cparams("parallel", "arbitrary"),
        name="gated_merge",
    )(y_pool, y_attn, wp, wa, z2, z2)


def _proj_res_kernel(m_ref, w_ref, x_ref, o_ref):
    o_ref[...] = x_ref[...] + jnp.dot(m_ref[...], w_ref[...], preferred_element_type=F32)


def _proj_residual(m, w, x2d, *, tm=1024, tn=1024):
    T, K = m.shape
    D = w.shape[1]
    tm, tn = _tile(T, tm), _tile(D, tn)
    return pl.pallas_call(
        _proj_res_kernel,
        out_shape=jax.ShapeDtypeStruct((T, D), F32),
        grid=(T // tm, D // tn),
        in_specs=[pl.BlockSpec((tm, K), lambda i, j: (i, 0)),
                  pl.BlockSpec((K, tn), lambda i, j: (0, j)),
                  pl.BlockSpec((tm, tn), lambda i, j: (i, j))],
        out_specs=pl.BlockSpec((tm, tn), lambda i, j: (i, j)),
        compiler_params=_cparams("parallel", "arbitrary"),
        name="out_proj_residual",
    )(m, w, x2d)


def _top16_rows(scores):
    n, t = scores[0].shape
    rows = lax.broadcasted_iota(jnp.int32, (n, t), 0).astype(F32)
    krow = lax.broadcasted_iota(jnp.int32, (PEER_TOPK, t), 0)

    def body(k, carry):
        out = []
        for work, sv, si in carry:
            m = jnp.max(work, axis=0, keepdims=True)
            idx = jnp.min(jnp.where(work == m, rows, float(n)), axis=0, keepdims=True)
            out.append((jnp.where(rows == idx, -jnp.inf, work),
                        jnp.where(krow == k, m, sv),
                        jnp.where(krow == k, idx, si)))
        return tuple(out)

    zeros = jnp.zeros((PEER_TOPK, t), F32)
    res = lax.fori_loop(0, PEER_TOPK, body, tuple((s, zeros, zeros) for s in scores))
    return [(sv, si) for _, sv, si in res]


def _top16_pairs(sorted_pairs):
    K = PEER_TOPK
    t = sorted_pairs[0][0].shape[1]
    k2 = lax.broadcasted_iota(jnp.int32, (K, t), 0).astype(F32)
    flat = (k2,) + tuple(k2[:K // 2] + float(k1 * K) for k1 in range(1, K))
    cmax = tuple(sv0[0:1, :] + sv1[0:1, :] for sv0, sv1 in sorted_pairs)

    def body(_, carry):
        out = []
        for (cand, cnt, z), cm in zip(carry, cmax):
            m = jnp.max(cand[0], axis=0, keepdims=True)
            for c in cand[1:]:
                m = jnp.maximum(m, jnp.max(c, axis=0, keepdims=True))
            idx = None
            for c, f in zip(cand, flat):
                fi = jnp.min(jnp.where(c == m, f, float(K * K)), axis=0, keepdims=True)
                idx = fi if idx is None else jnp.minimum(idx, fi)
            cand = tuple(jnp.where(f == idx, -jnp.inf, c) for c, f in zip(cand, flat))
            cnt = jnp.where(k2 == jnp.floor(idx * (1.0 / K)), cnt + 1.0, cnt)
            out.append((cand, cnt, z + jnp.exp(m - cm)))
        return tuple(out)

    init = tuple(((sv0[0:1, :] + sv1,) + tuple(sv0[k1:k1 + 1, :] + sv1[:K // 2] for k1 in range(1, K)),
                  jnp.zeros((K, t), F32), jnp.zeros((1, t), F32)) for sv0, sv1 in sorted_pairs)
    res = lax.fori_loop(0, K, body, init)
    return [(cnt, z) for _, cnt, z in res]


def _route_kernel(qt_ref, sub_ref, g_ref, thr_ref, cf_ref, k1_ref, e1_ref):
    hi = lax.Precision.HIGHEST
    t = qt_ref.shape[1]
    lane_tiles = [slice(c * LANES, (c + 1) * LANES) for c in range(t // LANES)]
    for h in range(PEER_HEADS):
        base = h * 2 * PEER_HALF
        scores = []
        for lt in lane_tiles:
            for half in range(2):
                rows = slice(base + half * PEER_HALF, base + (half + 1) * PEER_HALF)
                scores.append(jnp.dot(sub_ref[h, half], qt_ref[rows, lt],
                                      preferred_element_type=F32, precision=hi))
        tops = _top16_rows(scores)
        stairs = _top16_pairs([(tops[2 * c][0], tops[2 * c + 1][0]) for c in range(len(lane_tiles))])
        key_row = lax.broadcasted_iota(jnp.int32, (PEER_NKEYS, LANES), 0).astype(F32)
        for c, lt in enumerate(lane_tiles):
            s0, s1 = scores[2 * c], scores[2 * c + 1]
            (sv0, si0), (sv1, si1) = tops[2 * c], tops[2 * c + 1]
            cnt, z = stairs[c]
            paired = jnp.zeros_like(s0)
            key1 = jnp.full_like(s1, -float(PEER_TOPK))
            for k in range(PEER_TOPK):
                paired = jnp.where(key_row == si0[k:k + 1, :], cnt[k:k + 1, :], paired)
                key1 = jnp.where(key_row == si1[k:k + 1, :], -float(k), key1)
            thr_ref[h, :, lt] = 1.0 - paired
            cf_ref[h, :, lt] = jnp.exp(s0 - sv0[0:1, :]) / z
            e1_ref[h, :, lt] = jnp.exp(s1 - sv1[0:1, :]).astype(e1_ref.dtype)
            k1_ref[h, :, lt] = key1.astype(k1_ref.dtype)

    zero = jnp.zeros((), BF16)

    def expert_rows(i1, _):
        gate = None
        for h in range(PEER_HEADS):
            thr = thr_ref[h, pl.ds(i1, 1), :].astype(BF16)
            cf = cf_ref[h, pl.ds(i1, 1), :].astype(BF16)
            g = cf * jnp.where(k1_ref[h] >= thr, e1_ref[h], zero)
            gate = g if gate is None else gate + g
        rows = pl.ds(pl.multiple_of(i1 * PEER_NKEYS, PEER_NKEYS), PEER_NKEYS)
        for c, lt in enumerate(lane_tiles):
            g_ref[c, rows, :] = gate[:, lt]
        return 0

    lax.fori_loop(0, PEER_NKEYS, expert_rows, 0)


def _peer_route(qt, subkeys, *, t=256):
    Q, T = qt.shape
    t = _tile(T, t)
    E = PEER_NKEYS * PEER_NKEYS
    scratch = pltpu.VMEM((PEER_HEADS, PEER_NKEYS, t), F32)
    scratch_bf = pltpu.VMEM((PEER_HEADS, PEER_NKEYS, t), BF16)
    return pl.pallas_call(
        _route_kernel,
        out_shape=jax.ShapeDtypeStruct((T // LANES, E, LANES), BF16),
        grid=(T // t,),
        in_specs=[pl.BlockSpec((Q, t), lambda i: (0, i)),
                  pl.BlockSpec(subkeys.shape, lambda i: (0, 0, 0, 0))],
        out_specs=pl.BlockSpec((t // LANES, E, LANES), lambda i: (i, 0, 0)),
        scratch_shapes=[scratch, scratch, scratch_bf, scratch_bf],
        compiler_params=_cparams("parallel"),
        name="peer_route",
    )(qt, subkeys)


def _peer_kernel(hn_ref, u_ref, vt_ref, g_ref, o_ref):
    e = pl.program_id(1)

    @pl.when(e == 0)
    def _zero():
        o_ref[...] = jnp.zeros_like(o_ref)

    a = jnp.dot(u_ref[...], hn_ref[...], preferred_element_type=F32)
    cols = []
    for c in range(g_ref.shape[0]):
        ac = a[:, c * LANES:(c + 1) * LANES]
        gelu = 0.5 * ac * (1.0 + lax.erf(ac * (2.0 ** -0.5)))
        cols.append((gelu * g_ref[c].astype(F32)).astype(BF16))
    act = jnp.concatenate(cols, axis=1) if len(cols) > 1 else cols[0]
    o_ref[...] += jnp.dot(vt_ref[...], act, preferred_element_type=F32)


def _peer_experts(hn_t, u_bf, vt_bf, g_tiles, *, tm=512, te=512):
    D, T = hn_t.shape
    E = u_bf.shape[0]
    tm, te = _tile(T, tm), _tile(E, te)
    return pl.pallas_call(
        _peer_kernel,
        out_shape=jax.ShapeDtypeStruct((D, T), F32),
        grid=(T // tm, E // te),
        in_specs=[pl.BlockSpec((D, tm), lambda i, e: (0, i)),
                  pl.BlockSpec((te, D), lambda i, e: (e, 0)),
                  pl.BlockSpec((D, te), lambda i, e: (0, e)),
                  pl.BlockSpec((tm // LANES, te, LANES), lambda i, e: (i, e, 0))],
        out_specs=pl.BlockSpec((D, tm), lambda i, e: (0, i)),
        compiler_params=_cparams("parallel", "arbitrary"),
        name="peer_experts",
    )(hn_t, u_bf, vt_bf, g_tiles)


def _add_t_kernel(h_ref, ot_ref, o_ref):
    o_ref[...] = h_ref[...] + ot_ref[...].T


def _add_transposed(h, out_t, *, tm=256):
    T, D = h.shape
    tm = _tile(T, tm)
    return pl.pallas_call(
        _add_t_kernel,
        out_shape=jax.ShapeDtypeStruct((T, D), F32),
        grid=(T // tm,),
        in_specs=[pl.BlockSpec((tm, D), lambda i: (i, 0)),
                  pl.BlockSpec((D, tm), lambda i: (0, i))],
        out_specs=pl.BlockSpec((tm, D), lambda i: (i, 0)),
        compiler_params=_cparams("parallel"),
        name="peer_residual",
    )(h, out_t)


def _rope_tables(S):
    half = HEAD_DIM // 2
    inv = ROPE_THETA ** (-jnp.arange(half, dtype=F32) / half)
    ang = jnp.arange(S).astype(F32)[:, None] * inv[None, :]
    cos, sin = jnp.cos(ang), jnp.sin(ang)
    return jnp.concatenate([cos, cos], axis=-1), jnp.concatenate([-sin, sin], axis=-1)


def kernel(x, norm_mix, w_in, pool_w, pool_scale, q_norm, k_norm, w_pool_out,
           w_attn_out, w_o, norm_ffn, peer_wq, peer_subkeys, peer_u, peer_v):
    B, S, D = x.shape
    T = B * S
    depth = w_in.shape[0]
    pool_width = POOL_GROUPS * pool_w.shape[-1]
    attn_width = w_attn_out.shape[1]
    n_heads = attn_width // HEAD_DIM
    in_width = w_in.shape[-1]
    assert in_width == pool_width + 3 * attn_width + 2 * D
    assert S % MOBA_BLOCK == 0 and T % LANES == 0
    q_col = pool_width // HEAD_DIM
    k_col = q_col + n_heads
    v_col = k_col + n_heads
    gp_col = pool_width + 3 * attn_width
    ga_col = gp_col + D
    cos, sin_signed = _rope_tables(S)

    h = x.reshape(T, D)
    for l in range(depth):
        xn = _rmsnorm_bf16(h, norm_mix[l], transpose=False)
        z = _matmul(xn, w_in[l].astype(BF16), name="in_proj")
        z3 = z.reshape(B, S, in_width)
        y_pool = _pool_mixer(z3, pool_w[l].astype(BF16), pool_scale[l])
        y_attn = _moba_attention(z3, cos, sin_signed, q_norm[l], k_norm[l], n_heads=n_heads,
                                 q_col=q_col, k_col=k_col, v_col=v_col).reshape(T, attn_width)
        merged = _merge(y_pool, y_attn, w_pool_out[l].astype(BF16), w_attn_out[l].astype(BF16), z,
                        gp_col=gp_col, ga_col=ga_col)
        h = _proj_residual(merged, w_o[l].astype(BF16), h)

        hn_t = _rmsnorm_bf16(h, norm_ffn[l], transpose=True)
        qt = _matmul(peer_wq[l].T.astype(BF16), hn_t, name="peer_query")
        g_tiles = _peer_route(qt, peer_subkeys[l])
        out_t = _peer_experts(hn_t, peer_u[l].astype(BF16), peer_v[l].T.astype(BF16), g_tiles)
        h = _add_transposed(h, out_t)
    return h.reshape(B, S, D)
```

```python
import functools

import jax
import jax.numpy as jnp
from jax import lax
from jax.experimental import pallas as pl
from jax.experimental.pallas import tpu as pltpu

POOL_WINDOWS = (2, 4, 8, 16)
POOL_GROUPS = len(POOL_WINDOWS)
POOL_HALO = 16
HEAD_DIM = 128
MOBA_BLOCK = 256
MOBA_TOPK = 3
MOBA_CHUNK = 4
MOBA_HEADS_PER_STEP = 2
ROPE_THETA = 10000.0
PEER_HEADS = 8
PEER_NKEYS = 128
PEER_HALF = 128
PEER_TOPK = 16
EPS = 1e-6
NEG = -1e30
LOG2_E = 1.4426950408889634

LANES = 128
SUBLANES = 8
VMEM_LIMIT_BYTES = 56 * 1024 * 1024

F32 = jnp.float32
BF16 = jnp.bfloat16


def _cparams(*sem):
    return pltpu.CompilerParams(dimension_semantics=sem, vmem_limit_bytes=VMEM_LIMIT_BYTES)


def _tile(n, want):
    t = min(n, want)
    assert n % t == 0, (n, t)
    return t


def _rms_kernel(x_ref, g_ref, o_ref, *, transpose):
    x = x_ref[...]
    r = lax.rsqrt(jnp.mean(x * x, axis=-1, keepdims=True) + EPS)
    y = x * r * g_ref[...]
    if transpose:
        y = y.T
    o_ref[...] = y.astype(o_ref.dtype)


def _rmsnorm_bf16(x2d, g, *, transpose, tm=256):
    T, D = x2d.shape
    tm = _tile(T, tm)
    if transpose:
        out_shape = jax.ShapeDtypeStruct((D, T), BF16)
        out_spec = pl.BlockSpec((D, tm), lambda i: (0, i))
    else:
        out_shape = jax.ShapeDtypeStruct((T, D), BF16)
        out_spec = pl.BlockSpec((tm, D), lambda i: (i, 0))
    return pl.pallas_call(
        functools.partial(_rms_kernel, transpose=transpose),
        out_shape=out_shape,
        grid=(T // tm,),
        in_specs=[pl.BlockSpec((tm, D), lambda i: (i, 0)),
                  pl.BlockSpec((1, D), lambda i: (0, 0))],
        out_specs=out_spec,
        compiler_params=_cparams("parallel"),
        name="rmsnorm_cast",
    )(x2d, g.reshape(1, D))


def _mm_kernel(a_ref, b_ref, o_ref):
    o_ref[...] = jnp.dot(a_ref[...], b_ref[...], preferred_element_type=F32).astype(o_ref.dtype)


def _matmul(a, b, *, tm=1024, tn=1024, out_dtype=F32, name="matmul"):
    M, K = a.shape
    _, N = b.shape
    tm, tn = _tile(M, tm), _tile(N, tn)
    return pl.pallas_call(
        _mm_kernel,
        out_shape=jax.ShapeDtypeStruct((M, N), out_dtype),
        grid=(M // tm, N // tn),
        in_specs=[pl.BlockSpec((tm, K), lambda i, j: (i, 0)),
                  pl.BlockSpec((K, tn), lambda i, j: (0, j))],
        out_specs=pl.BlockSpec((tm, tn), lambda i, j: (i, j)),
        compiler_params=_cparams("parallel", "arbitrary"),
        name=name,
    )(a, b)


def _pool_kernel(x_ref, halo_ref, pw_ref, ps_ref, o_ref, *, ts, C):
    i = pl.program_id(1)
    pos = i * ts + lax.broadcasted_iota(jnp.int32, (ts, 1), 0)
    for g, w in enumerate(POOL_WINDOWS):
        x = x_ref[0, :, g * C:(g + 1) * C]
        halo = halo_ref[0, :, g * C:(g + 1) * C]
        halo = jnp.where(i == 0, 0.0, halo)
        s = jnp.concatenate([halo, x], axis=0)
        d = 1
        while d < w:
            s = s + pltpu.roll(s, d, axis=0)
            d *= 2
        s = s[POOL_HALO:]
        cnt = jnp.minimum(pos + 1, w).astype(F32)
        y = (s / cnt - x).astype(BF16)
        o = jnp.dot(y, pw_ref[g], preferred_element_type=F32) * ps_ref[:, g * C:(g + 1) * C]
        o_ref[:, g * C:(g + 1) * C] = o.astype(o_ref.dtype)


def _pool_mixer(z3, pool_w_bf, pool_scale, *, ts=512):
    B, S, _ = z3.shape
    G, C, _ = pool_w_bf.shape
    W = G * C
    ts = _tile(S, ts)
    hb = ts // POOL_HALO
    n_s = S // ts
    return pl.pallas_call(
        functools.partial(_pool_kernel, ts=ts, C=C),
        out_shape=jax.ShapeDtypeStruct((B * S, W), BF16),
        grid=(B, S // ts),
        in_specs=[pl.BlockSpec((1, ts, W), lambda b, i: (b, i, 0)),
                  pl.BlockSpec((1, POOL_HALO, W), lambda b, i: (b, jnp.maximum(i * hb - 1, 0), 0)),
                  pl.BlockSpec((G, C, C), lambda b, i: (0, 0, 0)),
                  pl.BlockSpec((1, W), lambda b, i: (0, 0))],
        out_specs=pl.BlockSpec((ts, W), lambda b, i: (b * n_s + i, 0)),
        compiler_params=_cparams("parallel", "parallel"),
        name="pool_mixer",
    )(z3, z3, pool_w_bf, pool_scale.reshape(1, W))


def _norm_rope(t, gain, cos, sin_signed):
    r = lax.rsqrt(jnp.mean(t * t, axis=-1, keepdims=True) + EPS)
    t = t * r * gain
    return t * cos + pltpu.roll(t, HEAD_DIM // 2, axis=1) * sin_signed


def _moba_kernel(zq_ref, zk_ref, zv_ref, cos_ref, sin_ref, qg_ref, kg_ref, o_ref,
                 kr_ref, vt_ref, km_ref, bias_ref, s_ref, *, nb, ch, hps):
    L = MOBA_BLOCK
    hd = HEAD_DIM
    i = pl.program_id(1)
    heads = [slice(hh * hd, (hh + 1) * hd) for hh in range(hps)]

    @pl.when(i == 0)
    def _prepare_keys_values():
        for hh, hs in enumerate(heads):
            for j in range(nb):
                rows = slice(j * L, (j + 1) * L)
                part = slice((j % ch) * L, (j % ch + 1) * L)
                k = _norm_rope(zk_ref[0, rows, hs], kg_ref[...], cos_ref[rows, :], sin_ref[rows, :])
                kr_ref[hh, j // ch, part, :] = k.astype(BF16)
                km_ref[hh, j:j + 1, :] = jnp.mean(k, axis=0, keepdims=True)
                vt_ref[hh, j // ch, :, part] = zv_ref[0, rows, hs].T.astype(BF16)

    q0 = pl.multiple_of(i * L, L)
    cos_q, sin_q = cos_ref[pl.ds(q0, L), :], sin_ref[pl.ds(q0, L), :]
    blk = lax.broadcasted_iota(jnp.int32, (nb, L), 0)
    qsb = []
    for hh, hs in enumerate(heads):
        qt = _norm_rope(zq_ref[0, :, hs], qg_ref[...], cos_q, sin_q).T
        qsb.append((qt * (hd ** -0.5 * LOG2_E)).astype(BF16))
        gate = jnp.dot(km_ref[hh], qt, preferred_element_type=F32,
                       precision=lax.Precision.HIGHEST)
        gate = jnp.where(blk < i, gate, NEG)
        bias = jnp.full((nb, L), NEG, F32)
        for _ in range(min(MOBA_TOPK, nb)):
            m = jnp.max(gate, axis=0, keepdims=True)
            idx = jnp.min(jnp.where(gate == m, blk, nb), axis=0, keepdims=True)
            hit = blk == idx
            bias = jnp.where(hit & (idx < i), 0.0, bias)
            gate = jnp.where(hit, -jnp.inf, gate)
        bias_ref[hh] = bias

    kpos = lax.broadcasted_iota(jnp.int32, (L, L), 0)
    qpos = lax.broadcasted_iota(jnp.int32, (L, L), 1)
    causal_bias = jnp.where(kpos <= qpos, 0.0, NEG)

    def fold(x, op):
        parts = [x[r * SUBLANES:(r + 1) * SUBLANES] for r in range(L // SUBLANES)]
        while len(parts) > 1:
            parts = [op(parts[k], parts[k + 1]) for k in range(0, len(parts), 2)]
        return parts[0]

    n_chunks = i // ch + 1

    def score_chunk(c, mx):
        out = []
        for hh in range(hps):
            s = jnp.dot(kr_ref[hh, c], qsb[hh], preferred_element_type=F32)
            m = mx[hh]
            for b in range(ch):
                j = c * ch + b
                bj = jnp.where(j == i, causal_bias, bias_ref[hh, pl.ds(j, 1), :])
                sb = s[b * L:(b + 1) * L] + bj
                s_ref[hh, pl.ds(pl.multiple_of(j * L, L), L), :] = sb
                m = jnp.maximum(m, fold(sb, jnp.maximum))
            out.append(m)
        return tuple(out)

    mx = lax.fori_loop(0, n_chunks, score_chunk,
                       tuple(jnp.full((SUBLANES, L), NEG, F32) for _ in range(hps)))
    mq = [jnp.max(m, axis=0, keepdims=True) for m in mx]

    def value_chunk(c, carry):
        out = []
        for hh in range(hps):
            ps, acc = carry[hh]
            for b in range(ch):
                j = c * ch + b
                p = jnp.exp2(s_ref[hh, pl.ds(pl.multiple_of(j * L, L), L), :] - mq[hh])
                ps = ps + fold(p, jnp.add)
                acc = acc + jnp.dot(vt_ref[hh, c, :, b * L:(b + 1) * L], p.astype(BF16),
                                    preferred_element_type=F32)
            out.append((ps, acc))
        return tuple(out)

    init = tuple((jnp.zeros((SUBLANES, L), F32), jnp.zeros((hd, L), F32)) for _ in range(hps))
    res = lax.fori_loop(0, n_chunks, value_chunk, init)
    for hs, (ps, acc) in zip(heads, res):
        o_ref[0, :, hs] = (acc / jnp.sum(ps, axis=0, keepdims=True)).T.astype(o_ref.dtype)


def _moba_attention(z3, cos, sin_signed, q_gain, k_gain, *, n_heads, q_col, k_col, v_col,
                    chunk=MOBA_CHUNK, hps=MOBA_HEADS_PER_STEP):
    B, S, _ = z3.shape
    L = MOBA_BLOCK
    nb = S // L
    hd = HEAD_DIM
    ch = chunk
    while nb % ch:
        ch //= 2
    assert n_heads % hps == 0 and q_col % hps == 0 and k_col % hps == 0 and v_col % hps == 0
    g = n_heads // hps
    w = hps * hd
    return pl.pallas_call(
        functools.partial(_moba_kernel, nb=nb, ch=ch, hps=hps),
        out_shape=jax.ShapeDtypeStruct((B, S, n_heads * hd), BF16),
        grid=(B * g, nb),
        in_specs=[pl.BlockSpec((1, L, w), lambda bg, i: (bg // g, i, q_col // hps + bg % g)),
                  pl.BlockSpec((1, S, w), lambda bg, i: (bg // g, 0, k_col // hps + bg % g)),
                  pl.BlockSpec((1, S, w), lambda bg, i: (bg // g, 0, v_col // hps + bg % g)),
                  pl.BlockSpec((S, hd), lambda bg, i: (0, 0)),
                  pl.BlockSpec((S, hd), lambda bg, i: (0, 0)),
                  pl.BlockSpec((1, hd), lambda bg, i: (0, 0)),
                  pl.BlockSpec((1, hd), lambda bg, i: (0, 0))],
        out_specs=pl.BlockSpec((1, L, w), lambda bg, i: (bg // g, i, bg % g)),
        scratch_shapes=[pltpu.VMEM((hps, nb // ch, ch * L, hd), BF16),
                        pltpu.VMEM((hps, nb // ch, hd, ch * L), BF16),
                        pltpu.VMEM((hps, nb, hd), F32),
                        pltpu.VMEM((hps, nb, L), F32),
                        pltpu.VMEM((hps, S, L), F32)],
        compiler_params=_cparams("parallel", "arbitrary"),
        name="moba_attention",
    )(z3, z3, z3, cos, sin_signed, q_gain.reshape(1, hd), k_gain.reshape(1, hd))


def _merge_kernel(yp_ref, ya_ref, wp_ref, wa_ref, gp_ref, ga_ref, o_ref):
    p = jnp.dot(yp_ref[...], wp_ref[...], preferred_element_type=F32)
    a = jnp.dot(ya_ref[...], wa_ref[...], preferred_element_type=F32)
    o_ref[...] = (jax.nn.sigmoid(gp_ref[...]) * p + jax.nn.sigmoid(ga_ref[...]) * a).astype(o_ref.dtype)


def _merge(y_pool, y_attn, wp, wa, z2, *, gp_col, ga_col, tm=512, tn=1024):
    T, K = y_pool.shape
    D = wp.shape[1]
    tm, tn = _tile(T, tm), _tile(D, tn)
    gp_blk, ga_blk = gp_col // tn, ga_col // tn
    return pl.pallas_call(
        _merge_kernel,
        out_shape=jax.ShapeDtypeStruct((T, D), BF16),
        grid=(T // tm, D // tn),
        in_specs=[pl.BlockSpec((tm, K), lambda i, j: (i, 0)),
                  pl.BlockSpec((tm, K), lambda i, j: (i, 0)),
                  pl.BlockSpec((K, tn), lambda i, j: (0, j)),
                  pl.BlockSpec((K, tn), lambda i, j: (0, j)),
                  pl.BlockSpec((tm, tn), lambda i, j: (i, gp_blk + j)),
                  pl.BlockSpec((tm, tn), lambda i, j: (i, ga_blk + j))],
        out_specs=pl.BlockSpec((tm, tn), lambda i, j: (i, j)),
        compiler_params=_cparams("parallel", "arbitrary"),
        name="gated_merge",
    )(y_pool, y_attn, wp, wa, z2, z2)


def _proj_res_kernel(m_ref, w_ref, x_ref, o_ref):
    o_ref[...] = x_ref[...] + jnp.dot(m_ref[...], w_ref[...], preferred_element_type=F32)


def _proj_residual(m, w, x2d, *, tm=1024, tn=1024):
    T, K = m.shape
    D = w.shape[1]
    tm, tn = _tile(T, tm), _tile(D, tn)
    return pl.pallas_call(
        _proj_res_kernel,
        out_shape=jax.ShapeDtypeStruct((T, D), F32),
        grid=(T // tm, D // tn),
        in_specs=[pl.BlockSpec((tm, K), lambda i, j: (i, 0)),
                  pl.BlockSpec((K, tn), lambda i, j: (0, j)),
                  pl.BlockSpec((tm, tn), lambda i, j: (i, j))],
        out_specs=pl.BlockSpec((tm, tn), lambda i, j: (i, j)),
        compiler_params=_cparams("parallel", "arbitrary"),
        name="out_proj_residual",
    )(m, w, x2d)


def _top16_rows(scores):
    n, t = scores[0].shape
    rows = lax.broadcasted_iota(jnp.int32, (n, t), 0).astype(F32)
    krow = lax.broadcasted_iota(jnp.int32, (PEER_TOPK, t), 0)

    def body(k, carry):
        out = []
        for work, sv, si in carry:
            m = jnp.max(work, axis=0, keepdims=True)
            idx = jnp.min(jnp.where(work == m, rows, float(n)), axis=0, keepdims=True)
            out.append((jnp.where(rows == idx, -jnp.inf, work),
                        jnp.where(krow == k, m, sv),
                        jnp.where(krow == k, idx, si)))
        return tuple(out)

    zeros = jnp.zeros((PEER_TOPK, t), F32)
    res = lax.fori_loop(0, PEER_TOPK, body, tuple((s, zeros, zeros) for s in scores))
    return [(sv, si) for _, sv, si in res]


def _top16_pairs(sorted_pairs):
    K = PEER_TOPK
    t = sorted_pairs[0][0].shape[1]
    k2 = lax.broadcasted_iota(jnp.int32, (K, t), 0).astype(F32)
    flat = (k2,) + tuple(k2[:K // 2] + float(k1 * K) for k1 in range(1, K))
    cmax = tuple(sv0[0:1, :] + sv1[0:1, :] for sv0, sv1 in sorted_pairs)

    def body(_, carry):
        out = []
        for (cand, cnt, z), cm in zip(carry, cmax):
            m = jnp.max(cand[0], axis=0, keepdims=True)
            for c in cand[1:]:
                m = jnp.maximum(m, jnp.max(c, axis=0, keepdims=True))
            idx = None
            for c, f in zip(cand, flat):
                fi = jnp.min(jnp.where(c == m, f, float(K * K)), axis=0, keepdims=True)
                idx = fi if idx is None else jnp.minimum(idx, fi)
            cand = tuple(jnp.where(f == idx, -jnp.inf, c) for c, f in zip(cand, flat))
            cnt = jnp.where(k2 == jnp.floor(idx * (1.0 / K)), cnt + 1.0, cnt)
            out.append((cand, cnt, z + jnp.exp(m - cm)))
        return tuple(out)

    init = tuple(((sv0[0:1, :] + sv1,) + tuple(sv0[k1:k1 + 1, :] + sv1[:K // 2] for k1 in range(1, K)),
                  jnp.zeros((K, t), F32), jnp.zeros((1, t), F32)) for sv0, sv1 in sorted_pairs)
    res = lax.fori_loop(0, K, body, init)
    return [(cnt, z) for _, cnt, z in res]


def _count_ge(x, bound):
    return jnp.sum(jnp.where(x >= bound, 1.0, 0.0), axis=0, keepdims=True)


def _top16_values(scores):
    t = scores[0].shape[1]
    krow = lax.broadcasted_iota(jnp.int32, (PEER_TOPK, t), 0)

    def body(k, carry):
        out = []
        for work, sv in carry:
            m = jnp.max(work, axis=0, keepdims=True)
            out.append((jnp.where(work == m, -jnp.inf, work), jnp.where(krow == k, m, sv)))
        return tuple(out)

    zeros = jnp.zeros((PEER_TOPK, t), F32)
    res = lax.fori_loop(0, PEER_TOPK, body, tuple((s, zeros) for s in scores))
    return [sv for _, sv in res]


def _pair_sums(sv0, sv1):
    K = PEER_TOPK
    return (sv0[0:1, :] + sv1,) + tuple(sv0[k1:k1 + 1, :] + sv1[:K // 2] for k1 in range(1, K))


def _top16_pairs_distinct(sorted_pairs):
    K = PEER_TOPK
    t = sorted_pairs[0][0].shape[1]
    krow = lax.broadcasted_iota(jnp.int32, (K, t), 0)
    cmax = tuple(sv0[0:1, :] + sv1[0:1, :] for sv0, sv1 in sorted_pairs)

    def body(_, carry):
        out = []
        for (cand, _, z), cm in zip(carry, cmax):
            m = jnp.max(cand[0], axis=0, keepdims=True)
            for c in cand[1:]:
                m = jnp.maximum(m, jnp.max(c, axis=0, keepdims=True))
            out.append((tuple(jnp.where(c == m, -jnp.inf, c) for c in cand), m, z + jnp.exp(m - cm)))
        return tuple(out)

    zero = jnp.zeros((1, t), F32)
    res = lax.fori_loop(0, K, body, tuple((_pair_sums(sv0, sv1), zero, zero) for sv0, sv1 in sorted_pairs))
    out = []
    for (sv0, sv1), (_, kth, z) in zip(sorted_pairs, res):
        cnt = jnp.zeros((K, t), F32)
        for k1, c in enumerate(_pair_sums(sv0, sv1)):
            cnt = jnp.where(krow == k1, _count_ge(c, kth), cnt)
        distinct = jnp.sum(cnt, axis=0, keepdims=True) == float(K)
        out.append((cnt, z, distinct))
    return out


def _route_head(h, qt_ref, sub_ref, thr_ref, cf_ref, k1_ref, e1_ref, lane_tiles, exact):
    hi = lax.Precision.HIGHEST
    K = PEER_TOPK
    base = h * 2 * PEER_HALF
    scores = []
    for lt in lane_tiles:
        for half in range(2):
            rows = slice(base + half * PEER_HALF, base + (half + 1) * PEER_HALF)
            scores.append(jnp.dot(sub_ref[h, half], qt_ref[rows, lt],
                                  preferred_element_type=F32, precision=hi))
    n = len(lane_tiles)
    if exact:
        tops = _top16_rows(scores)
        svs = [sv for sv, _ in tops]
        stairs = [(cnt, z, None) for cnt, z in _top16_pairs([(svs[2 * c], svs[2 * c + 1]) for c in range(n)])]
        key_row = lax.broadcasted_iota(jnp.int32, (PEER_NKEYS, LANES), 0).astype(F32)
        match = [[key_row == si[k:k + 1, :] for k in range(K)] for _, si in tops]
    else:
        svs = _top16_values(scores)
        stairs = _top16_pairs_distinct([(svs[2 * c], svs[2 * c + 1]) for c in range(n)])
        match = [[s == sv[k:k + 1, :] for k in range(K)] for s, sv in zip(scores, svs)]
    ok = None
    for c, lt in enumerate(lane_tiles):
        s0, s1 = scores[2 * c], scores[2 * c + 1]
        sv0, sv1 = svs[2 * c], svs[2 * c + 1]
        cnt, z, distinct = stairs[c]
        paired = jnp.zeros_like(s0)
        key1 = jnp.full_like(s1, -float(K))
        for k in range(K):
            paired = jnp.where(match[2 * c][k], cnt[k:k + 1, :], paired)
            key1 = jnp.where(match[2 * c + 1][k], -float(k), key1)
        thr_ref[h, :, lt] = 1.0 - paired
        cf_ref[h, :, lt] = jnp.exp(s0 - sv0[0:1, :]) / z
        e1_ref[h, :, lt] = jnp.exp(s1 - sv1[0:1, :]).astype(e1_ref.dtype)
        k1_ref[h, :, lt] = key1.astype(k1_ref.dtype)
        if not exact:
            good = (distinct & (_count_ge(s0, sv0[K - 1:K, :]) == float(K))
                    & (_count_ge(s1, sv1[K - 1:K, :]) == float(K)))
            ok = good if ok is None else ok & good
    return ok


def _route_kernel(qt_ref, sub_ref, g_ref, thr_ref, cf_ref, k1_ref, e1_ref):
    t = qt_ref.shape[1]
    lane_tiles = [slice(c * LANES, (c + 1) * LANES) for c in range(t // LANES)]
    refs = (qt_ref, sub_ref, thr_ref, cf_ref, k1_ref, e1_ref, lane_tiles)

    ok = None
    for h in range(PEER_HEADS):
        good = _route_head(h, *refs, exact=False)
        ok = good if ok is None else ok & good
    all_distinct = jnp.min(jnp.where(ok, 1.0, 0.0)) > 0.0

    @pl.when(jnp.logical_not(all_distinct))
    def _redo_with_tie_breaks():
        for h in range(PEER_HEADS):
            _route_head(h, *refs, exact=True)

    zero = jnp.zeros((), BF16)

    def expert_rows(i1, _):
        gate = None
        for h in range(PEER_HEADS):
            thr = thr_ref[h, pl.ds(i1, 1), :].astype(BF16)
            cf = cf_ref[h, pl.ds(i1, 1), :].astype(BF16)
            g = cf * jnp.where(k1_ref[h] >= thr, e1_ref[h], zero)
            gate = g if gate is None else gate + g
        rows = pl.ds(pl.multiple_of(i1 * PEER_NKEYS, PEER_NKEYS), PEER_NKEYS)
        for c, lt in enumerate(lane_tiles):
            g_ref[c, rows, :] = gate[:, lt]
        return 0

    lax.fori_loop(0, PEER_NKEYS, expert_rows, 0)


def _peer_route(qt, subkeys, *, t=256):
    Q, T = qt.shape
    t = _tile(T, t)
    E = PEER_NKEYS * PEER_NKEYS
    scratch = pltpu.VMEM((PEER_HEADS, PEER_NKEYS, t), F32)
    scratch_bf = pltpu.VMEM((PEER_HEADS, PEER_NKEYS, t), BF16)
    return pl.pallas_call(
        _route_kernel,
        out_shape=jax.ShapeDtypeStruct((T // LANES, E, LANES), BF16),
        grid=(T // t,),
        in_specs=[pl.BlockSpec((Q, t), lambda i: (0, i)),
                  pl.BlockSpec(subkeys.shape, lambda i: (0, 0, 0, 0))],
        out_specs=pl.BlockSpec((t // LANES, E, LANES), lambda i: (i, 0, 0)),
        scratch_shapes=[scratch, scratch, scratch_bf, scratch_bf],
        compiler_params=_cparams("parallel"),
        name="peer_route",
    )(qt, subkeys)


def _peer_kernel(hn_ref, u_ref, v_ref, g_ref, h_ref, o_ref):
    e = pl.program_id(1)

    @pl.when(e == 0)
    def _residual():
        o_ref[...] = h_ref[...]

    a = jnp.dot(u_ref[...], hn_ref[...], preferred_element_type=F32)
    cols = []
    for c in range(g_ref.shape[0]):
        ac = a[:, c * LANES:(c + 1) * LANES]
        gelu = 0.5 * ac * (1.0 + lax.erf(ac * (2.0 ** -0.5)))
        cols.append((gelu * g_ref[c].astype(F32)).astype(BF16))
    act = jnp.concatenate(cols, axis=1) if len(cols) > 1 else cols[0]
    o_ref[...] += lax.dot_general(act, v_ref[...], (((0,), (0,)), ((), ())),
                                  preferred_element_type=F32)


def _peer_experts(hn_t, u_bf, v_bf, g_tiles, h, *, tm=512, te=512):
    D, T = hn_t.shape
    E = u_bf.shape[0]
    tm, te = _tile(T, tm), _tile(E, te)
    return pl.pallas_call(
        _peer_kernel,
        out_shape=jax.ShapeDtypeStruct((T, D), F32),
        grid=(T // tm, E // te),
        in_specs=[pl.BlockSpec((D, tm), lambda i, e: (0, i)),
                  pl.BlockSpec((te, D), lambda i, e: (e, 0)),
                  pl.BlockSpec((te, D), lambda i, e: (e, 0)),
                  pl.BlockSpec((tm // LANES, te, LANES), lambda i, e: (i, e, 0)),
                  pl.BlockSpec((tm, D), lambda i, e: (i, 0), pipeline_mode=pl.Buffered(1))],
        out_specs=pl.BlockSpec((tm, D), lambda i, e: (i, 0)),
        compiler_params=_cparams("parallel", "arbitrary"),
        name="peer_experts",
    )(hn_t, u_bf, v_bf, g_tiles, h)


def _rope_tables(S):
    half = HEAD_DIM // 2
    inv = ROPE_THETA ** (-jnp.arange(half, dtype=F32) / half)
    ang = jnp.arange(S).astype(F32)[:, None] * inv[None, :]
    cos, sin = jnp.cos(ang), jnp.sin(ang)
    return jnp.concatenate([cos, cos], axis=-1), jnp.concatenate([-sin, sin], axis=-1)


def kernel(x, norm_mix, w_in, pool_w, pool_scale, q_norm, k_norm, w_pool_out,
           w_attn_out, w_o, norm_ffn, peer_wq, peer_subkeys, peer_u, peer_v):
    B, S, D = x.shape
    T = B * S
    depth = w_in.shape[0]
    pool_width = POOL_GROUPS * pool_w.shape[-1]
    attn_width = w_attn_out.shape[1]
    n_heads = attn_width // HEAD_DIM
    in_width = w_in.shape[-1]
    assert in_width == pool_width + 3 * attn_width + 2 * D
    assert S % MOBA_BLOCK == 0 and T % LANES == 0
    q_col = pool_width // HEAD_DIM
    k_col = q_col + n_heads
    v_col = k_col + n_heads
    gp_col = pool_width + 3 * attn_width
    ga_col = gp_col + D
    cos, sin_signed = _rope_tables(S)

    h = x.reshape(T, D)
    for l in range(depth):
        xn = _rmsnorm_bf16(h, norm_mix[l], transpose=False)
        z = _matmul(xn, w_in[l].astype(BF16), name="in_proj")
        z3 = z.reshape(B, S, in_width)
        y_pool = _pool_mixer(z3, pool_w[l].astype(BF16), pool_scale[l])
        y_attn = _moba_attention(z3, cos, sin_signed, q_norm[l], k_norm[l], n_heads=n_heads,
                                 q_col=q_col, k_col=k_col, v_col=v_col).reshape(T, attn_width)
        merged = _merge(y_pool, y_attn, w_pool_out[l].astype(BF16), w_attn_out[l].astype(BF16), z,
                        gp_col=gp_col, ga_col=ga_col)
        h = _proj_residual(merged, w_o[l].astype(BF16), h)

        hn_t = _rmsnorm_bf16(h, norm_ffn[l], transpose=True)
        qt = _matmul(peer_wq[l].T.astype(BF16), hn_t, name="peer_query")
        g_tiles = _peer_route(qt, peer_subkeys[l])
        h = _peer_experts(hn_t, peer_u[l].astype(BF16), peer_v[l].astype(BF16), g_tiles, h)
    return h.reshape(B, S, D)
```

```python
import functools

import jax
import jax.numpy as jnp
from jax import lax
from jax.experimental import pallas as pl
from jax.experimental.pallas import tpu as pltpu

POOL_WINDOWS = (2, 4, 8, 16)
POOL_GROUPS = len(POOL_WINDOWS)
POOL_HALO = 16
HEAD_DIM = 128
MOBA_BLOCK = 256
MOBA_TOPK = 3
MOBA_CHUNK = 4
MOBA_HEADS_PER_STEP = 2
ROPE_THETA = 10000.0
PEER_HEADS = 8
PEER_NKEYS = 128
PEER_HALF = 128
PEER_TOPK = 16
EPS = 1e-6
NEG = -1e30
LOG2_E = 1.4426950408889634

LANES = 128
SUBLANES = 8
VMEM_LIMIT_BYTES = 56 * 1024 * 1024

F32 = jnp.float32
BF16 = jnp.bfloat16


def _cparams(*sem):
    return pltpu.CompilerParams(dimension_semantics=sem, vmem_limit_bytes=VMEM_LIMIT_BYTES)


def _tile(n, want):
    t = min(n, want)
    assert n % t == 0, (n, t)
    return t


def _rms_kernel(x_ref, g_ref, o_ref, *, transpose):
    x = x_ref[...]
    r = lax.rsqrt(jnp.mean(x * x, axis=-1, keepdims=True) + EPS)
    y = x * r * g_ref[...]
    if transpose:
        y = y.T
    o_ref[...] = y.astype(o_ref.dtype)


def _rmsnorm_bf16(x2d, g, *, transpose, tm=256):
    T, D = x2d.shape
    tm = _tile(T, tm)
    if transpose:
        out_shape = jax.ShapeDtypeStruct((D, T), BF16)
        out_spec = pl.BlockSpec((D, tm), lambda i: (0, i))
    else:
        out_shape = jax.ShapeDtypeStruct((T, D), BF16)
        out_spec = pl.BlockSpec((tm, D), lambda i: (i, 0))
    return pl.pallas_call(
        functools.partial(_rms_kernel, transpose=transpose),
        out_shape=out_shape,
        grid=(T // tm,),
        in_specs=[pl.BlockSpec((tm, D), lambda i: (i, 0)),
                  pl.BlockSpec((1, D), lambda i: (0, 0))],
        out_specs=out_spec,
        compiler_params=_cparams("parallel"),
        name="rmsnorm_cast",
    )(x2d, g.reshape(1, D))


def _mm_kernel(a_ref, b_ref, o_ref):
    o_ref[...] = jnp.dot(a_ref[...], b_ref[...], preferred_element_type=F32).astype(o_ref.dtype)


def _matmul(a, b, *, tm=1024, tn=1024, out_dtype=F32, name="matmul"):
    M, K = a.shape
    _, N = b.shape
    tm, tn = _tile(M, tm), _tile(N, tn)
    return pl.pallas_call(
        _mm_kernel,
        out_shape=jax.ShapeDtypeStruct((M, N), out_dtype),
        grid=(M // tm, N // tn),
        in_specs=[pl.BlockSpec((tm, K), lambda i, j: (i, 0)),
                  pl.BlockSpec((K, tn), lambda i, j: (0, j))],
        out_specs=pl.BlockSpec((tm, tn), lambda i, j: (i, j)),
        compiler_params=_cparams("parallel", "arbitrary"),
        name=name,
    )(a, b)


def _pool_kernel(x_ref, halo_ref, pw_ref, ps_ref, o_ref, *, ts, C):
    i = pl.program_id(1)
    pos = i * ts + lax.broadcasted_iota(jnp.int32, (ts, 1), 0)
    for g, w in enumerate(POOL_WINDOWS):
        x = x_ref[0, :, g * C:(g + 1) * C]
        halo = halo_ref[0, :, g * C:(g + 1) * C]
        halo = jnp.where(i == 0, 0.0, halo)
        s = jnp.concatenate([halo, x], axis=0)
        d = 1
        while d < w:
            s = s + pltpu.roll(s, d, axis=0)
            d *= 2
        s = s[POOL_HALO:]
        cnt = jnp.minimum(pos + 1, w).astype(F32)
        y = (s / cnt - x).astype(BF16)
        o = jnp.dot(y, pw_ref[g], preferred_element_type=F32) * ps_ref[:, g * C:(g + 1) * C]
        o_ref[:, g * C:(g + 1) * C] = o.astype(o_ref.dtype)


def _pool_mixer(z3, pool_w_bf, pool_scale, *, ts=512):
    B, S, _ = z3.shape
    G, C, _ = pool_w_bf.shape
    W = G * C
    ts = _tile(S, ts)
    hb = ts // POOL_HALO
    n_s = S // ts
    return pl.pallas_call(
        functools.partial(_pool_kernel, ts=ts, C=C),
        out_shape=jax.ShapeDtypeStruct((B * S, W), BF16),
        grid=(B, S // ts),
        in_specs=[pl.BlockSpec((1, ts, W), lambda b, i: (b, i, 0)),
                  pl.BlockSpec((1, POOL_HALO, W), lambda b, i: (b, jnp.maximum(i * hb - 1, 0), 0)),
                  pl.BlockSpec((G, C, C), lambda b, i: (0, 0, 0)),
                  pl.BlockSpec((1, W), lambda b, i: (0, 0))],
        out_specs=pl.BlockSpec((ts, W), lambda b, i: (b * n_s + i, 0)),
        compiler_params=_cparams("parallel", "parallel"),
        name="pool_mixer",
    )(z3, z3, pool_w_bf, pool_scale.reshape(1, W))


def _norm_rope(t, gain, cos, sin_signed):
    r = lax.rsqrt(jnp.mean(t * t, axis=-1, keepdims=True) + EPS)
    t = t * r * gain
    return t * cos + pltpu.roll(t, HEAD_DIM // 2, axis=1) * sin_signed


def _moba_kernel(zq_ref, zk_ref, zv_ref, cos_ref, sin_ref, qg_ref, kg_ref, o_ref,
                 kr_ref, vt_ref, km_ref, bias_ref, s_ref, *, nb, ch, hps):
    L = MOBA_BLOCK
    hd = HEAD_DIM
    i = pl.program_id(1)
    heads = [slice(hh * hd, (hh + 1) * hd) for hh in range(hps)]

    @pl.when(i == 0)
    def _prepare_keys_values():
        for hh, hs in enumerate(heads):
            for j in range(nb):
                rows = slice(j * L, (j + 1) * L)
                part = slice((j % ch) * L, (j % ch + 1) * L)
                k = _norm_rope(zk_ref[0, rows, hs], kg_ref[...], cos_ref[rows, :], sin_ref[rows, :])
                kr_ref[hh, j // ch, part, :] = k.astype(BF16)
                km_ref[hh, j:j + 1, :] = jnp.mean(k, axis=0, keepdims=True)
                vt_ref[hh, j // ch, :, part] = zv_ref[0, rows, hs].T.astype(BF16)

    q0 = pl.multiple_of(i * L, L)
    cos_q, sin_q = cos_ref[pl.ds(q0, L), :], sin_ref[pl.ds(q0, L), :]
    blk = lax.broadcasted_iota(jnp.int32, (nb, L), 0)
    qsb = []
    for hh, hs in enumerate(heads):
        qt = _norm_rope(zq_ref[0, :, hs], qg_ref[...], cos_q, sin_q).T
        qsb.append((qt * (hd ** -0.5 * LOG2_E)).astype(BF16))
        gate = jnp.dot(km_ref[hh], qt, preferred_element_type=F32,
                       precision=lax.Precision.HIGHEST)
        gate = jnp.where(blk < i, gate, NEG)
        bias = jnp.full((nb, L), NEG, F32)
        for _ in range(min(MOBA_TOPK, nb)):
            m = jnp.max(gate, axis=0, keepdims=True)
            idx = jnp.min(jnp.where(gate == m, blk, nb), axis=0, keepdims=True)
            hit = blk == idx
            bias = jnp.where(hit & (idx < i), 0.0, bias)
            gate = jnp.where(hit, -jnp.inf, gate)
        bias_ref[hh] = bias

    kpos = lax.broadcasted_iota(jnp.int32, (L, L), 0)
    qpos = lax.broadcasted_iota(jnp.int32, (L, L), 1)
    causal_bias = jnp.where(kpos <= qpos, 0.0, NEG)

    def fold(x, op):
        parts = [x[r * SUBLANES:(r + 1) * SUBLANES] for r in range(L // SUBLANES)]
        while len(parts) > 1:
            parts = [op(parts[k], parts[k + 1]) for k in range(0, len(parts), 2)]
        return parts[0]

    n_chunks = i // ch + 1

    def score_chunk(c, mx):
        out = []
        for hh in range(hps):
            s = jnp.dot(kr_ref[hh, c], qsb[hh], preferred_element_type=F32)
            m = mx[hh]
            for b in range(ch):
                j = c * ch + b
                bj = jnp.where(j == i, causal_bias, bias_ref[hh, pl.ds(j, 1), :])
                sb = s[b * L:(b + 1) * L] + bj
                s_ref[hh, pl.ds(pl.multiple_of(j * L, L), L), :] = sb
                m = jnp.maximum(m, fold(sb, jnp.maximum))
            out.append(m)
        return tuple(out)

    mx = lax.fori_loop(0, n_chunks, score_chunk,
                       tuple(jnp.full((SUBLANES, L), NEG, F32) for _ in range(hps)))
    mq = [jnp.max(m, axis=0, keepdims=True) for m in mx]

    def value_chunk(c, carry):
        out = []
        for hh in range(hps):
            ps, acc = carry[hh]
            for b in range(ch):
                j = c * ch + b
                p = jnp.exp2(s_ref[hh, pl.ds(pl.multiple_of(j * L, L), L), :] - mq[hh])
                ps = ps + fold(p, jnp.add)
                acc = acc + jnp.dot(vt_ref[hh, c, :, b * L:(b + 1) * L], p.astype(BF16),
                                    preferred_element_type=F32)
            out.append((ps, acc))
        return tuple(out)

    init = tuple((jnp.zeros((SUBLANES, L), F32), jnp.zeros((hd, L), F32)) for _ in range(hps))
    res = lax.fori_loop(0, n_chunks, value_chunk, init)
    for hs, (ps, acc) in zip(heads, res):
        o_ref[0, :, hs] = (acc / jnp.sum(ps, axis=0, keepdims=True)).T.astype(o_ref.dtype)


def _moba_attention(z3, cos, sin_signed, q_gain, k_gain, *, n_heads, q_col, k_col, v_col,
                    chunk=MOBA_CHUNK, hps=MOBA_HEADS_PER_STEP):
    B, S, _ = z3.shape
    L = MOBA_BLOCK
    nb = S // L
    hd = HEAD_DIM
    ch = chunk
    while nb % ch:
        ch //= 2
    assert n_heads % hps == 0 and q_col % hps == 0 and k_col % hps == 0 and v_col % hps == 0
    g = n_heads // hps
    w = hps * hd
    return pl.pallas_call(
        functools.partial(_moba_kernel, nb=nb, ch=ch, hps=hps),
        out_shape=jax.ShapeDtypeStruct((B, S, n_heads * hd), BF16),
        grid=(B * g, nb),
        in_specs=[pl.BlockSpec((1, L, w), lambda bg, i: (bg // g, i, q_col // hps + bg % g)),
                  pl.BlockSpec((1, S, w), lambda bg, i: (bg // g, 0, k_col // hps + bg % g)),
                  pl.BlockSpec((1, S, w), lambda bg, i: (bg // g, 0, v_col // hps + bg % g)),
                  pl.BlockSpec((S, hd), lambda bg, i: (0, 0)),
                  pl.BlockSpec((S, hd), lambda bg, i: (0, 0)),
                  pl.BlockSpec((1, hd), lambda bg, i: (0, 0)),
                  pl.BlockSpec((1, hd), lambda bg, i: (0, 0))],
        out_specs=pl.BlockSpec((1, L, w), lambda bg, i: (bg // g, i, bg % g)),
        scratch_shapes=[pltpu.VMEM((hps, nb // ch, ch * L, hd), BF16),
                        pltpu.VMEM((hps, nb // ch, hd, ch * L), BF16),
                        pltpu.VMEM((hps, nb, hd), F32),
                        pltpu.VMEM((hps, nb, L), F32),
                        pltpu.VMEM((hps, S, L), F32)],
        compiler_params=_cparams("parallel", "arbitrary"),
        name="moba_attention",
    )(z3, z3, z3, cos, sin_signed, q_gain.reshape(1, hd), k_gain.reshape(1, hd))


def _merge_kernel(yp_ref, ya_ref, wp_ref, wa_ref, gp_ref, ga_ref, o_ref):
    p = jnp.dot(yp_ref[...], wp_ref[...], preferred_element_type=F32)
    a = jnp.dot(ya_ref[...], wa_ref[...], preferred_element_type=F32)
    o_ref[...] = (jax.nn.sigmoid(gp_ref[...]) * p + jax.nn.sigmoid(ga_ref[...]) * a).astype(o_ref.dtype)


def _merge(y_pool, y_attn, wp, wa, z2, *, gp_col, ga_col, tm=512, tn=1024):
    T, K = y_pool.shape
    D = wp.shape[1]
    tm, tn = _tile(T, tm), _tile(D, tn)
    gp_blk, ga_blk = gp_col // tn, ga_col // tn
    return pl.pallas_call(
        _merge_kernel,
        out_shape=jax.ShapeDtypeStruct((T, D), BF16),
        grid=(T // tm, D // tn),
        in_specs=[pl.BlockSpec((tm, K), lambda i, j: (i, 0)),
                  pl.BlockSpec((tm, K), lambda i, j: (i, 0)),
                  pl.BlockSpec((K, tn), lambda i, j: (0, j)),
                  pl.BlockSpec((K, tn), lambda i, j: (0, j)),
                  pl.BlockSpec((tm, tn), lambda i, j: (i, gp_blk + j)),
                  pl.BlockSpec((tm, tn), lambda i, j: (i, ga_blk + j))],
        out_specs=pl.BlockSpec((tm, tn), lambda i, j: (i, j)),
        compiler_params=_cparams("parallel", "arbitrary"),
        name="gated_merge",
    )(y_pool, y_attn, wp, wa, z2, z2)


def _proj_res_kernel(m_ref, w_ref, x_ref, o_ref):
    o_ref[...] = x_ref[...] + jnp.dot(m_ref[...], w_ref[...], preferred_element_type=F32)


def _proj_residual(m, w, x2d, *, tm=1024, tn=1024):
    T, K = m.shape
    D = w.shape[1]
    tm, tn = _tile(T, tm), _tile(D, tn)
    return pl.pallas_call(
        _proj_res_kernel,
        out_shape=jax.ShapeDtypeStruct((T, D), F32),
        grid=(T // tm, D // tn),
        in_specs=[pl.BlockSpec((tm, K), lambda i, j: (i, 0)),
                  pl.BlockSpec((K, tn), lambda i, j: (0, j)),
                  pl.BlockSpec((tm, tn), lambda i, j: (i, j))],
        out_specs=pl.BlockSpec((tm, tn), lambda i, j: (i, j)),
        compiler_params=_cparams("parallel", "arbitrary"),
        name="out_proj_residual",
    )(m, w, x2d)


def _top16_rows(scores):
    n, t = scores[0].shape
    rows = lax.broadcasted_iota(jnp.int32, (n, t), 0).astype(F32)
    krow = lax.broadcasted_iota(jnp.int32, (PEER_TOPK, t), 0)

    def body(k, carry):
        out = []
        for work, sv, si in carry:
            m = jnp.max(work, axis=0, keepdims=True)
            idx = jnp.min(jnp.where(work == m, rows, float(n)), axis=0, keepdims=True)
            out.append((jnp.where(rows == idx, -jnp.inf, work),
                        jnp.where(krow == k, m, sv),
                        jnp.where(krow == k, idx, si)))
        return tuple(out)

    zeros = jnp.zeros((PEER_TOPK, t), F32)
    res = lax.fori_loop(0, PEER_TOPK, body, tuple((s, zeros, zeros) for s in scores))
    return [(sv, si) for _, sv, si in res]


def _top16_pairs(sorted_pairs):
    K = PEER_TOPK
    t = sorted_pairs[0][0].shape[1]
    k2 = lax.broadcasted_iota(jnp.int32, (K, t), 0).astype(F32)
    flat = (k2,) + tuple(k2[:K // 2] + float(k1 * K) for k1 in range(1, K))
    cmax = tuple(sv0[0:1, :] + sv1[0:1, :] for sv0, sv1 in sorted_pairs)

    def body(_, carry):
        out = []
        for (cand, cnt, z), cm in zip(carry, cmax):
            m = jnp.max(cand[0], axis=0, keepdims=True)
            for c in cand[1:]:
                m = jnp.maximum(m, jnp.max(c, axis=0, keepdims=True))
            idx = None
            for c, f in zip(cand, flat):
                fi = jnp.min(jnp.where(c == m, f, float(K * K)), axis=0, keepdims=True)
                idx = fi if idx is None else jnp.minimum(idx, fi)
            cand = tuple(jnp.where(f == idx, -jnp.inf, c) for c, f in zip(cand, flat))
            cnt = jnp.where(k2 == jnp.floor(idx * (1.0 / K)), cnt + 1.0, cnt)
            out.append((cand, cnt, z + jnp.exp(m - cm)))
        return tuple(out)

    init = tuple(((sv0[0:1, :] + sv1,) + tuple(sv0[k1:k1 + 1, :] + sv1[:K // 2] for k1 in range(1, K)),
                  jnp.zeros((K, t), F32), jnp.zeros((1, t), F32)) for sv0, sv1 in sorted_pairs)
    res = lax.fori_loop(0, K, body, init)
    return [(cnt, z) for _, cnt, z in res]


def _count_ge(x, bound):
    return jnp.sum(jnp.where(x >= bound, 1.0, 0.0), axis=0, keepdims=True)


def _top16_values(scores):
    t = scores[0].shape[1]
    krow = lax.broadcasted_iota(jnp.int32, (PEER_TOPK, t), 0)

    def body(k, carry):
        out = []
        for work, sv in carry:
            m = jnp.max(work, axis=0, keepdims=True)
            out.append((jnp.where(work == m, -jnp.inf, work), jnp.where(krow == k, m, sv)))
        return tuple(out)

    zeros = jnp.zeros((PEER_TOPK, t), F32)
    res = lax.fori_loop(0, PEER_TOPK, body, tuple((s, zeros) for s in scores))
    return [sv for _, sv in res]


def _pair_sums(sv0, sv1):
    K = PEER_TOPK
    return (sv0[0:1, :] + sv1,) + tuple(sv0[k1:k1 + 1, :] + sv1[:K // 2] for k1 in range(1, K))


def _top16_pairs_distinct(sorted_pairs):
    K = PEER_TOPK
    t = sorted_pairs[0][0].shape[1]
    krow = lax.broadcasted_iota(jnp.int32, (K, t), 0)
    cmax = tuple(sv0[0:1, :] + sv1[0:1, :] for sv0, sv1 in sorted_pairs)

    def body(_, carry):
        out = []
        for (cand, _, z), cm in zip(carry, cmax):
            m = jnp.max(cand[0], axis=0, keepdims=True)
            for c in cand[1:]:
                m = jnp.maximum(m, jnp.max(c, axis=0, keepdims=True))
            out.append((tuple(jnp.where(c == m, -jnp.inf, c) for c in cand), m, z + jnp.exp(m - cm)))
        return tuple(out)

    zero = jnp.zeros((1, t), F32)
    res = lax.fori_loop(0, K, body, tuple((_pair_sums(sv0, sv1), zero, zero) for sv0, sv1 in sorted_pairs))
    out = []
    for (sv0, sv1), (_, kth, z) in zip(sorted_pairs, res):
        cnt = jnp.zeros((K, t), F32)
        for k1, c in enumerate(_pair_sums(sv0, sv1)):
            cnt = jnp.where(krow == k1, _count_ge(c, kth), cnt)
        distinct = jnp.sum(cnt, axis=0, keepdims=True) == float(K)
        out.append((cnt, z, distinct))
    return out


def _route_head(h, qt_ref, sub_ref, thr_ref, cf_ref, k1_ref, e1_ref, lane_tiles, exact):
    hi = lax.Precision.HIGHEST
    K = PEER_TOPK
    base = h * 2 * PEER_HALF
    scores = []
    for lt in lane_tiles:
        for half in range(2):
            rows = slice(base + half * PEER_HALF, base + (half + 1) * PEER_HALF)
            scores.append(jnp.dot(sub_ref[h, half], qt_ref[rows, lt],
                                  preferred_element_type=F32, precision=hi))
    n = len(lane_tiles)
    if exact:
        tops = _top16_rows(scores)
        svs = [sv for sv, _ in tops]
        stairs = [(cnt, z, None) for cnt, z in _top16_pairs([(svs[2 * c], svs[2 * c + 1]) for c in range(n)])]
        key_row = lax.broadcasted_iota(jnp.int32, (PEER_NKEYS, LANES), 0).astype(F32)
        match = [[key_row == si[k:k + 1, :] for k in range(K)] for _, si in tops]
    else:
        svs = _top16_values(scores)
        stairs = _top16_pairs_distinct([(svs[2 * c], svs[2 * c + 1]) for c in range(n)])
        match = [[s == sv[k:k + 1, :] for k in range(K)] for s, sv in zip(scores, svs)]
    ok = None
    for c, lt in enumerate(lane_tiles):
        s0, s1 = scores[2 * c], scores[2 * c + 1]
        sv0, sv1 = svs[2 * c], svs[2 * c + 1]
        cnt, z, distinct = stairs[c]
        paired = jnp.zeros_like(s0)
        key1 = jnp.full_like(s1, -float(K))
        for k in range(K):
            paired = jnp.where(match[2 * c][k], cnt[k:k + 1, :], paired)
            key1 = jnp.where(match[2 * c + 1][k], -float(k), key1)
        thr_ref[h, :, lt] = 1.0 - paired
        cf_ref[h, :, lt] = jnp.exp(s0 - sv0[0:1, :]) / z
        e1_ref[h, :, lt] = jnp.exp(s1 - sv1[0:1, :]).astype(e1_ref.dtype)
        k1_ref[h, :, lt] = key1.astype(k1_ref.dtype)
        if not exact:
            good = (distinct & (_count_ge(s0, sv0[K - 1:K, :]) == float(K))
                    & (_count_ge(s1, sv1[K - 1:K, :]) == float(K)))
            ok = good if ok is None else ok & good
    return ok


def _route_kernel(qt_ref, sub_ref, g_ref, thr_ref, cf_ref, k1_ref, e1_ref):
    t = qt_ref.shape[1]
    lane_tiles = [slice(c * LANES, (c + 1) * LANES) for c in range(t // LANES)]
    refs = (qt_ref, sub_ref, thr_ref, cf_ref, k1_ref, e1_ref, lane_tiles)

    for h in range(PEER_HEADS):
        ok = _route_head(h, *refs, exact=False)
        all_distinct = jnp.min(jnp.where(ok, 1.0, 0.0)) > 0.0

        @pl.when(jnp.logical_not(all_distinct))
        def _redo_with_tie_breaks():
            _route_head(h, *refs, exact=True)

    zero = jnp.zeros((), BF16)

    def expert_rows(i1, _):
        gate = None
        for h in range(PEER_HEADS):
            thr = thr_ref[h, pl.ds(i1, 1), :].astype(BF16)
            cf = cf_ref[h, pl.ds(i1, 1), :].astype(BF16)
            g = cf * jnp.where(k1_ref[h] >= thr, e1_ref[h], zero)
            gate = g if gate is None else gate + g
        rows = pl.ds(pl.multiple_of(i1 * PEER_NKEYS, PEER_NKEYS), PEER_NKEYS)
        for c, lt in enumerate(lane_tiles):
            g_ref[c, rows, :] = gate[:, lt]
        return 0

    lax.fori_loop(0, PEER_NKEYS, expert_rows, 0)


def _peer_route(qt, subkeys, *, t=256):
    Q, T = qt.shape
    t = _tile(T, t)
    E = PEER_NKEYS * PEER_NKEYS
    scratch = pltpu.VMEM((PEER_HEADS, PEER_NKEYS, t), F32)
    scratch_bf = pltpu.VMEM((PEER_HEADS, PEER_NKEYS, t), BF16)
    return pl.pallas_call(
        _route_kernel,
        out_shape=jax.ShapeDtypeStruct((T // LANES, E, LANES), BF16),
        grid=(T // t,),
        in_specs=[pl.BlockSpec((Q, t), lambda i: (0, i)),
                  pl.BlockSpec(subkeys.shape, lambda i: (0, 0, 0, 0))],
        out_specs=pl.BlockSpec((t // LANES, E, LANES), lambda i: (i, 0, 0)),
        scratch_shapes=[scratch, scratch, scratch_bf, scratch_bf],
        compiler_params=_cparams("parallel"),
        name="peer_route",
    )(qt, subkeys)


def _peer_kernel(hn_ref, u_ref, v_ref, g_ref, h_ref, o_ref):
    e = pl.program_id(1)

    @pl.when(e == 0)
    def _residual():
        o_ref[...] = h_ref[...]

    a = jnp.dot(u_ref[...], hn_ref[...], preferred_element_type=F32)
    cols = []
    for c in range(g_ref.shape[0]):
        ac = a[:, c * LANES:(c + 1) * LANES]
        gelu = 0.5 * ac * (1.0 + lax.erf(ac * (2.0 ** -0.5)))
        cols.append((gelu * g_ref[c].astype(F32)).astype(BF16))
    act = jnp.concatenate(cols, axis=1) if len(cols) > 1 else cols[0]
    o_ref[...] += lax.dot_general(act, v_ref[...], (((0,), (0,)), ((), ())),
                                  preferred_element_type=F32)


def _peer_experts(hn_t, u_bf, v_bf, g_tiles, h, *, tm=512, te=512):
    D, T = hn_t.shape
    E = u_bf.shape[0]
    tm, te = _tile(T, tm), _tile(E, te)
    return pl.pallas_call(
        _peer_kernel,
        out_shape=jax.ShapeDtypeStruct((T, D), F32),
        grid=(T // tm, E // te),
        in_specs=[pl.BlockSpec((D, tm), lambda i, e: (0, i)),
                  pl.BlockSpec((te, D), lambda i, e: (e, 0)),
                  pl.BlockSpec((te, D), lambda i, e: (e, 0)),
                  pl.BlockSpec((tm // LANES, te, LANES), lambda i, e: (i, e, 0)),
                  pl.BlockSpec((tm, D), lambda i, e: (i, 0), pipeline_mode=pl.Buffered(1))],
        out_specs=pl.BlockSpec((tm, D), lambda i, e: (i, 0)),
        compiler_params=_cparams("parallel", "arbitrary"),
        name="peer_experts",
    )(hn_t, u_bf, v_bf, g_tiles, h)


def _rope_tables(S):
    half = HEAD_DIM // 2
    inv = ROPE_THETA ** (-jnp.arange(half, dtype=F32) / half)
    ang = jnp.arange(S).astype(F32)[:, None] * inv[None, :]
    cos, sin = jnp.cos(ang), jnp.sin(ang)
    return jnp.concatenate([cos, cos], axis=-1), jnp.concatenate([-sin, sin], axis=-1)


def kernel(x, norm_mix, w_in, pool_w, pool_scale, q_norm, k_norm, w_pool_out,
           w_attn_out, w_o, norm_ffn, peer_wq, peer_subkeys, peer_u, peer_v):
    B, S, D = x.shape
    T = B * S
    depth = w_in.shape[0]
    pool_width = POOL_GROUPS * pool_w.shape[-1]
    attn_width = w_attn_out.shape[1]
    n_heads = attn_width // HEAD_DIM
    in_width = w_in.shape[-1]
    assert in_width == pool_width + 3 * attn_width + 2 * D
    assert S % MOBA_BLOCK == 0 and T % LANES == 0
    q_col = pool_width // HEAD_DIM
    k_col = q_col + n_heads
    v_col = k_col + n_heads
    gp_col = pool_width + 3 * attn_width
    ga_col = gp_col + D
    cos, sin_signed = _rope_tables(S)

    h = x.reshape(T, D)
    for l in range(depth):
        xn = _rmsnorm_bf16(h, norm_mix[l], transpose=False)
        z = _matmul(xn, w_in[l].astype(BF16), name="in_proj")
        z3 = z.reshape(B, S, in_width)
        y_pool = _pool_mixer(z3, pool_w[l].astype(BF16), pool_scale[l])
        y_attn = _moba_attention(z3, cos, sin_signed, q_norm[l], k_norm[l], n_heads=n_heads,
                                 q_col=q_col, k_col=k_col, v_col=v_col).reshape(T, attn_width)
        merged = _merge(y_pool, y_attn, w_pool_out[l].astype(BF16), w_attn_out[l].astype(BF16), z,
                        gp_col=gp_col, ga_col=ga_col)
        h = _proj_residual(merged, w_o[l].astype(BF16), h)

        hn_t = _rmsnorm_bf16(h, norm_ffn[l], transpose=True)
        qt = _matmul(peer_wq[l].T.astype(BF16), hn_t, name="peer_query")
        g_tiles = _peer_route(qt, peer_subkeys[l])
        h = _peer_experts(hn_t, peer_u[l].astype(BF16), peer_v[l].astype(BF16), g_tiles, h)
    return h.reshape(B, S, D)
```

```python
import functools

import jax
import jax.numpy as jnp
from jax import lax
from jax.experimental import pallas as pl
from jax.experimental.pallas import tpu as pltpu

POOL_WINDOWS = (2, 4, 8, 16)
POOL_GROUPS = len(POOL_WINDOWS)
POOL_HALO = 16
HEAD_DIM = 128
MOBA_BLOCK = 256
MOBA_TOPK = 3
MOBA_CHUNK = 4
MOBA_HEADS_PER_STEP = 2
ROPE_THETA = 10000.0
PEER_HEADS = 8
PEER_NKEYS = 128
PEER_HALF = 128
PEER_TOPK = 16
EPS = 1e-6
NEG = -1e30
LOG2_E = 1.4426950408889634

LANES = 128
SUBLANES = 8
VMEM_LIMIT_BYTES = 56 * 1024 * 1024

F32 = jnp.float32
BF16 = jnp.bfloat16


def _cparams(*sem):
    return pltpu.CompilerParams(dimension_semantics=sem, vmem_limit_bytes=VMEM_LIMIT_BYTES)


def _tile(n, want):
    t = min(n, want)
    assert n % t == 0, (n, t)
    return t


def _rms_kernel(x_ref, g_ref, o_ref, *, transpose):
    x = x_ref[...]
    r = lax.rsqrt(jnp.mean(x * x, axis=-1, keepdims=True) + EPS)
    y = x * r * g_ref[...]
    if transpose:
        y = y.T
    o_ref[...] = y.astype(o_ref.dtype)


def _rmsnorm_bf16(x2d, g, *, transpose, tm=256):
    T, D = x2d.shape
    tm = _tile(T, tm)
    if transpose:
        out_shape = jax.ShapeDtypeStruct((D, T), BF16)
        out_spec = pl.BlockSpec((D, tm), lambda i: (0, i))
    else:
        out_shape = jax.ShapeDtypeStruct((T, D), BF16)
        out_spec = pl.BlockSpec((tm, D), lambda i: (i, 0))
    return pl.pallas_call(
        functools.partial(_rms_kernel, transpose=transpose),
        out_shape=out_shape,
        grid=(T // tm,),
        in_specs=[pl.BlockSpec((tm, D), lambda i: (i, 0)),
                  pl.BlockSpec((1, D), lambda i: (0, 0))],
        out_specs=out_spec,
        compiler_params=_cparams("parallel"),
        name="rmsnorm_cast",
    )(x2d, g.reshape(1, D))


def _mm_kernel(a_ref, b_ref, o_ref):
    o_ref[...] = jnp.dot(a_ref[...], b_ref[...], preferred_element_type=F32).astype(o_ref.dtype)


def _matmul(a, b, *, tm=1024, tn=1024, out_dtype=F32, name="matmul"):
    M, K = a.shape
    _, N = b.shape
    tm, tn = _tile(M, tm), _tile(N, tn)
    return pl.pallas_call(
        _mm_kernel,
        out_shape=jax.ShapeDtypeStruct((M, N), out_dtype),
        grid=(M // tm, N // tn),
        in_specs=[pl.BlockSpec((tm, K), lambda i, j: (i, 0)),
                  pl.BlockSpec((K, tn), lambda i, j: (0, j))],
        out_specs=pl.BlockSpec((tm, tn), lambda i, j: (i, j)),
        compiler_params=_cparams("parallel", "arbitrary"),
        name=name,
    )(a, b)


def _pool_kernel(x_ref, halo_ref, pw_ref, ps_ref, o_ref, *, ts, C):
    i = pl.program_id(1)
    pos = i * ts + lax.broadcasted_iota(jnp.int32, (ts, 1), 0)
    for g, w in enumerate(POOL_WINDOWS):
        x = x_ref[0, :, g * C:(g + 1) * C]
        halo = halo_ref[0, :, g * C:(g + 1) * C]
        halo = jnp.where(i == 0, 0.0, halo)
        s = jnp.concatenate([halo, x], axis=0)
        d = 1
        while d < w:
            s = s + pltpu.roll(s, d, axis=0)
            d *= 2
        s = s[POOL_HALO:]
        cnt = jnp.minimum(pos + 1, w).astype(F32)
        y = (s / cnt - x).astype(BF16)
        o = jnp.dot(y, pw_ref[g], preferred_element_type=F32) * ps_ref[:, g * C:(g + 1) * C]
        o_ref[:, g * C:(g + 1) * C] = o.astype(o_ref.dtype)


def _pool_mixer(z3, pool_w_bf, pool_scale, *, ts=512):
    B, S, _ = z3.shape
    G, C, _ = pool_w_bf.shape
    W = G * C
    ts = _tile(S, ts)
    hb = ts // POOL_HALO
    n_s = S // ts
    return pl.pallas_call(
        functools.partial(_pool_kernel, ts=ts, C=C),
        out_shape=jax.ShapeDtypeStruct((B * S, W), BF16),
        grid=(B, S // ts),
        in_specs=[pl.BlockSpec((1, ts, W), lambda b, i: (b, i, 0)),
                  pl.BlockSpec((1, POOL_HALO, W), lambda b, i: (b, jnp.maximum(i * hb - 1, 0), 0)),
                  pl.BlockSpec((G, C, C), lambda b, i: (0, 0, 0)),
                  pl.BlockSpec((1, W), lambda b, i: (0, 0))],
        out_specs=pl.BlockSpec((ts, W), lambda b, i: (b * n_s + i, 0)),
        compiler_params=_cparams("parallel", "parallel"),
        name="pool_mixer",
    )(z3, z3, pool_w_bf, pool_scale.reshape(1, W))


def _norm_rope(t, gain, cos, sin_signed):
    r = lax.rsqrt(jnp.mean(t * t, axis=-1, keepdims=True) + EPS)
    t = t * r * gain
    return t * cos + pltpu.roll(t, HEAD_DIM // 2, axis=1) * sin_signed


def _moba_kernel(zq_ref, zk_ref, zv_ref, cos_ref, sin_ref, qg_ref, kg_ref, o_ref,
                 kr_ref, vt_ref, km_ref, bias_ref, s_ref, *, nb, ch, hps):
    L = MOBA_BLOCK
    hd = HEAD_DIM
    i = pl.program_id(1)
    heads = [slice(hh * hd, (hh + 1) * hd) for hh in range(hps)]

    @pl.when(i == 0)
    def _prepare_keys_values():
        for hh, hs in enumerate(heads):
            for j in range(nb):
                rows = slice(j * L, (j + 1) * L)
                part = slice((j % ch) * L, (j % ch + 1) * L)
                k = _norm_rope(zk_ref[0, rows, hs], kg_ref[...], cos_ref[rows, :], sin_ref[rows, :])
                kr_ref[hh, j // ch, part, :] = k.astype(BF16)
                km_ref[hh, j:j + 1, :] = jnp.mean(k, axis=0, keepdims=True)
                vt_ref[hh, j // ch, :, part] = zv_ref[0, rows, hs].T.astype(BF16)

    q0 = pl.multiple_of(i * L, L)
    cos_q, sin_q = cos_ref[pl.ds(q0, L), :], sin_ref[pl.ds(q0, L), :]
    blk = lax.broadcasted_iota(jnp.int32, (nb, L), 0)
    qsb = []
    for hh, hs in enumerate(heads):
        qt = _norm_rope(zq_ref[0, :, hs], qg_ref[...], cos_q, sin_q).T
        qsb.append((qt * (hd ** -0.5 * LOG2_E)).astype(BF16))
        gate = jnp.dot(km_ref[hh], qt, preferred_element_type=F32,
                       precision=lax.Precision.HIGHEST)
        gate = jnp.where(blk < i, gate, NEG)
        bias = jnp.full((nb, L), NEG, F32)
        for _ in range(min(MOBA_TOPK, nb)):
            m = jnp.max(gate, axis=0, keepdims=True)
            idx = jnp.min(jnp.where(gate == m, blk, nb), axis=0, keepdims=True)
            hit = blk == idx
            bias = jnp.where(hit & (idx < i), 0.0, bias)
            gate = jnp.where(hit, -jnp.inf, gate)
        bias_ref[hh] = bias

    kpos = lax.broadcasted_iota(jnp.int32, (L, L), 0)
    qpos = lax.broadcasted_iota(jnp.int32, (L, L), 1)
    causal_bias = jnp.where(kpos <= qpos, 0.0, NEG)

    def fold(x, op):
        parts = [x[r * SUBLANES:(r + 1) * SUBLANES] for r in range(L // SUBLANES)]
        while len(parts) > 1:
            parts = [op(parts[k], parts[k + 1]) for k in range(0, len(parts), 2)]
        return parts[0]

    n_chunks = i // ch + 1

    def score_chunk(c, mx):
        out = []
        for hh in range(hps):
            s = jnp.dot(kr_ref[hh, c], qsb[hh], preferred_element_type=F32)
            m = mx[hh]
            for b in range(ch):
                j = c * ch + b
                bj = jnp.where(j == i, causal_bias, bias_ref[hh, pl.ds(j, 1), :])
                sb = s[b * L:(b + 1) * L] + bj
                s_ref[hh, pl.ds(pl.multiple_of(j * L, L), L), :] = sb
                m = jnp.maximum(m, fold(sb, jnp.maximum))
            out.append(m)
        return tuple(out)

    mx = lax.fori_loop(0, n_chunks, score_chunk,
                       tuple(jnp.full((SUBLANES, L), NEG, F32) for _ in range(hps)))
    mq = [jnp.max(m, axis=0, keepdims=True) for m in mx]

    def value_chunk(c, carry):
        out = []
        for hh in range(hps):
            ps, acc = carry[hh]
            for b in range(ch):
                j = c * ch + b
                p = jnp.exp2(s_ref[hh, pl.ds(pl.multiple_of(j * L, L), L), :] - mq[hh])
                ps = ps + fold(p, jnp.add)
                acc = acc + jnp.dot(vt_ref[hh, c, :, b * L:(b + 1) * L], p.astype(BF16),
                                    preferred_element_type=F32)
            out.append((ps, acc))
        return tuple(out)

    init = tuple((jnp.zeros((SUBLANES, L), F32), jnp.zeros((hd, L), F32)) for _ in range(hps))
    res = lax.fori_loop(0, n_chunks, value_chunk, init)
    for hs, (ps, acc) in zip(heads, res):
        o_ref[0, :, hs] = (acc / jnp.sum(ps, axis=0, keepdims=True)).T.astype(o_ref.dtype)


def _moba_attention(z3, cos, sin_signed, q_gain, k_gain, *, n_heads, q_col, k_col, v_col,
                    chunk=MOBA_CHUNK, hps=MOBA_HEADS_PER_STEP):
    B, S, _ = z3.shape
    L = MOBA_BLOCK
    nb = S // L
    hd = HEAD_DIM
    ch = chunk
    while nb % ch:
        ch //= 2
    assert n_heads % hps == 0 and q_col % hps == 0 and k_col % hps == 0 and v_col % hps == 0
    g = n_heads // hps
    w = hps * hd
    return pl.pallas_call(
        functools.partial(_moba_kernel, nb=nb, ch=ch, hps=hps),
        out_shape=jax.ShapeDtypeStruct((B, S, n_heads * hd), BF16),
        grid=(B * g, nb),
        in_specs=[pl.BlockSpec((1, L, w), lambda bg, i: (bg // g, i, q_col // hps + bg % g)),
                  pl.BlockSpec((1, S, w), lambda bg, i: (bg // g, 0, k_col // hps + bg % g)),
                  pl.BlockSpec((1, S, w), lambda bg, i: (bg // g, 0, v_col // hps + bg % g)),
                  pl.BlockSpec((S, hd), lambda bg, i: (0, 0)),
                  pl.BlockSpec((S, hd), lambda bg, i: (0, 0)),
                  pl.BlockSpec((1, hd), lambda bg, i: (0, 0)),
                  pl.BlockSpec((1, hd), lambda bg, i: (0, 0))],
        out_specs=pl.BlockSpec((1, L, w), lambda bg, i: (bg // g, i, bg % g)),
        scratch_shapes=[pltpu.VMEM((hps, nb // ch, ch * L, hd), BF16),
                        pltpu.VMEM((hps, nb // ch, hd, ch * L), BF16),
                        pltpu.VMEM((hps, nb, hd), F32),
                        pltpu.VMEM((hps, nb, L), F32),
                        pltpu.VMEM((hps, S, L), F32)],
        compiler_params=_cparams("parallel", "arbitrary"),
        name="moba_attention",
    )(z3, z3, z3, cos, sin_signed, q_gain.reshape(1, hd), k_gain.reshape(1, hd))


def _merge_kernel(yp_ref, ya_ref, wp_ref, wa_ref, gp_ref, ga_ref, o_ref):
    p = jnp.dot(yp_ref[...], wp_ref[...], preferred_element_type=F32)
    a = jnp.dot(ya_ref[...], wa_ref[...], preferred_element_type=F32)
    o_ref[...] = (jax.nn.sigmoid(gp_ref[...]) * p + jax.nn.sigmoid(ga_ref[...]) * a).astype(o_ref.dtype)


def _merge(y_pool, y_attn, wp, wa, z2, *, gp_col, ga_col, tm=512, tn=1024):
    T, K = y_pool.shape
    D = wp.shape[1]
    tm, tn = _tile(T, tm), _tile(D, tn)
    gp_blk, ga_blk = gp_col // tn, ga_col // tn
    return pl.pallas_call(
        _merge_kernel,
        out_shape=jax.ShapeDtypeStruct((T, D), BF16),
        grid=(T // tm, D // tn),
        in_specs=[pl.BlockSpec((tm, K), lambda i, j: (i, 0)),
                  pl.BlockSpec((tm, K), lambda i, j: (i, 0)),
                  pl.BlockSpec((K, tn), lambda i, j: (0, j)),
                  pl.BlockSpec((K, tn), lambda i, j: (0, j)),
                  pl.BlockSpec((tm, tn), lambda i, j: (i, gp_blk + j)),
                  pl.BlockSpec((tm, tn), lambda i, j: (i, ga_blk + j))],
        out_specs=pl.BlockSpec((tm, tn), lambda i, j: (i, j)),
        compiler_params=_cparams("parallel", "arbitrary"),
        name="gated_merge",
    )(y_pool, y_attn, wp, wa, z2, z2)


def _proj_res_kernel(m_ref, w_ref, x_ref, o_ref):
    o_ref[...] = x_ref[...] + jnp.dot(m_ref[...], w_ref[...], preferred_element_type=F32)


def _proj_residual(m, w, x2d, *, tm=1024, tn=1024):
    T, K = m.shape
    D = w.shape[1]
    tm, tn = _tile(T, tm), _tile(D, tn)
    return pl.pallas_call(
        _proj_res_kernel,
        out_shape=jax.ShapeDtypeStruct((T, D), F32),
        grid=(T // tm, D // tn),
        in_specs=[pl.BlockSpec((tm, K), lambda i, j: (i, 0)),
                  pl.BlockSpec((K, tn), lambda i, j: (0, j)),
                  pl.BlockSpec((tm, tn), lambda i, j: (i, j))],
        out_specs=pl.BlockSpec((tm, tn), lambda i, j: (i, j)),
        compiler_params=_cparams("parallel", "arbitrary"),
        name="out_proj_residual",
    )(m, w, x2d)


def _top16_rows(scores):
    n, t = scores[0].shape
    rows = lax.broadcasted_iota(jnp.int32, (n, t), 0).astype(F32)
    krow = lax.broadcasted_iota(jnp.int32, (PEER_TOPK, t), 0)

    def body(k, carry):
        out = []
        for work, sv, si in carry:
            m = jnp.max(work, axis=0, keepdims=True)
            idx = jnp.min(jnp.where(work == m, rows, float(n)), axis=0, keepdims=True)
            out.append((jnp.where(rows == idx, -jnp.inf, work),
                        jnp.where(krow == k, m, sv),
                        jnp.where(krow == k, idx, si)))
        return tuple(out)

    zeros = jnp.zeros((PEER_TOPK, t), F32)
    res = lax.fori_loop(0, PEER_TOPK, body, tuple((s, zeros, zeros) for s in scores))
    return [(sv, si) for _, sv, si in res]


def _top16_pairs(sorted_pairs):
    K = PEER_TOPK
    t = sorted_pairs[0][0].shape[1]
    k2 = lax.broadcasted_iota(jnp.int32, (K, t), 0).astype(F32)
    flat = (k2,) + tuple(k2[:K // 2] + float(k1 * K) for k1 in range(1, K))
    cmax = tuple(sv0[0:1, :] + sv1[0:1, :] for sv0, sv1 in sorted_pairs)

    def body(_, carry):
        out = []
        for (cand, cnt, z), cm in zip(carry, cmax):
            m = jnp.max(cand[0], axis=0, keepdims=True)
            for c in cand[1:]:
                m = jnp.maximum(m, jnp.max(c, axis=0, keepdims=True))
            idx = None
            for c, f in zip(cand, flat):
                fi = jnp.min(jnp.where(c == m, f, float(K * K)), axis=0, keepdims=True)
                idx = fi if idx is None else jnp.minimum(idx, fi)
            cand = tuple(jnp.where(f == idx, -jnp.inf, c) for c, f in zip(cand, flat))
            cnt = jnp.where(k2 == jnp.floor(idx * (1.0 / K)), cnt + 1.0, cnt)
            out.append((cand, cnt, z + jnp.exp(m - cm)))
        return tuple(out)

    init = tuple(((sv0[0:1, :] + sv1,) + tuple(sv0[k1:k1 + 1, :] + sv1[:K // 2] for k1 in range(1, K)),
                  jnp.zeros((K, t), F32), jnp.zeros((1, t), F32)) for sv0, sv1 in sorted_pairs)
    res = lax.fori_loop(0, K, body, init)
    return [(cnt, z) for _, cnt, z in res]


def _count_ge(x, bound):
    return jnp.sum(jnp.where(x >= bound, 1.0, 0.0), axis=0, keepdims=True)


def _top16_values(scores):
    t = scores[0].shape[1]
    krow = lax.broadcasted_iota(jnp.int32, (PEER_TOPK, t), 0)

    def body(k, carry):
        out = []
        for work, sv in carry:
            m = jnp.max(work, axis=0, keepdims=True)
            out.append((jnp.where(work == m, -jnp.inf, work), jnp.where(krow == k, m, sv)))
        return tuple(out)

    zeros = jnp.zeros((PEER_TOPK, t), F32)
    res = lax.fori_loop(0, PEER_TOPK, body, tuple((s, zeros) for s in scores))
    return [sv for _, sv in res]


def _staircase_sums(sv0, sv1):
    K = PEER_TOPK
    t = sv0.shape[1]
    k1_all = lax.broadcasted_iota(jnp.int32, (K, t), 0)
    k1_low = k1_all[:K // 2]
    cols = [jnp.where(k1_all >= 2, sv0 + sv1[0:1, :], -jnp.inf)]
    for k2 in range(1, K // 3):
        rows_in = K // (k2 + 1)
        cols.append(jnp.where((k1_low >= 2) & (k1_low < rows_in), sv0[:K // 2] + sv1[k2:k2 + 1, :], -jnp.inf))
    return sv0[0:1, :] + sv1, sv0[1:2, :] + sv1[:K // 2], cols


def _top16_pairs_distinct(sorted_pairs):
    K = PEER_TOPK
    t = sorted_pairs[0][0].shape[1]
    k1_low = lax.broadcasted_iota(jnp.int32, (K // 2, t), 0)
    cmax = tuple(sv0[0:1, :] + sv1[0:1, :] for sv0, sv1 in sorted_pairs)

    def flatten(stairs):
        row0, row1, cols = stairs
        return (row0, row1) + tuple(cols)

    def body(_, carry):
        out = []
        for (cand, _, z), cm in zip(carry, cmax):
            m = jnp.max(cand[0], axis=0, keepdims=True)
            for c in cand[1:]:
                m = jnp.maximum(m, jnp.max(c, axis=0, keepdims=True))
            out.append((tuple(jnp.where(c == m, -jnp.inf, c) for c in cand), m, z + jnp.exp(m - cm)))
        return tuple(out)

    zero = jnp.zeros((1, t), F32)
    stairs = [_staircase_sums(sv0, sv1) for sv0, sv1 in sorted_pairs]
    res = lax.fori_loop(0, K, body, tuple((flatten(s), zero, zero) for s in stairs))
    out = []
    for (row0, row1, cols), (_, kth, z) in zip(stairs, res):
        ge = [jnp.where(c >= kth, 1.0, 0.0) for c in cols]
        low = ge[0][:K // 2]
        for g in ge[1:]:
            low = low + g
        low = jnp.where(k1_low == 0, _count_ge(row0, kth), jnp.where(k1_low == 1, _count_ge(row1, kth), low))
        cnt = jnp.concatenate([low, ge[0][K // 2:]], axis=0)
        distinct = jnp.sum(cnt, axis=0, keepdims=True) == float(K)
        out.append((cnt, z, distinct))
    return out


def _route_head(h, qt_ref, sub_ref, thr_ref, cf_ref, k1_ref, e1_ref, lane_tiles, exact):
    hi = lax.Precision.HIGHEST
    K = PEER_TOPK
    base = h * 2 * PEER_HALF
    scores = []
    for lt in lane_tiles:
        for half in range(2):
            rows = slice(base + half * PEER_HALF, base + (half + 1) * PEER_HALF)
            scores.append(jnp.dot(sub_ref[h, half], qt_ref[rows, lt],
                                  preferred_element_type=F32, precision=hi))
    n = len(lane_tiles)
    if exact:
        tops = _top16_rows(scores)
        svs = [sv for sv, _ in tops]
        stairs = [(cnt, z, None) for cnt, z in _top16_pairs([(svs[2 * c], svs[2 * c + 1]) for c in range(n)])]
        key_row = lax.broadcasted_iota(jnp.int32, (PEER_NKEYS, LANES), 0).astype(F32)
        match = [[key_row == si[k:k + 1, :] for k in range(K)] for _, si in tops]
    else:
        svs = _top16_values(scores)
        stairs = _top16_pairs_distinct([(svs[2 * c], svs[2 * c + 1]) for c in range(n)])
        match = [[s == sv[k:k + 1, :] for k in range(K)] for s, sv in zip(scores, svs)]
    ok = None
    for c, lt in enumerate(lane_tiles):
        s0, s1 = scores[2 * c], scores[2 * c + 1]
        sv0, sv1 = svs[2 * c], svs[2 * c + 1]
        cnt, z, distinct = stairs[c]
        paired = jnp.zeros_like(s0)
        key1 = jnp.full_like(s1, -float(K))
        for k in range(K):
            paired = jnp.where(match[2 * c][k], cnt[k:k + 1, :], paired)
            key1 = jnp.where(match[2 * c + 1][k], -float(k), key1)
        thr_ref[h, :, lt] = 1.0 - paired
        cf_ref[h, :, lt] = jnp.exp(s0 - sv0[0:1, :]) / z
        e1_ref[h, :, lt] = jnp.exp(s1 - sv1[0:1, :]).astype(e1_ref.dtype)
        k1_ref[h, :, lt] = key1.astype(k1_ref.dtype)
        if not exact:
            good = (distinct & (_count_ge(s0, sv0[K - 1:K, :]) == float(K))
                    & (_count_ge(s1, sv1[K - 1:K, :]) == float(K)))
            ok = good if ok is None else ok & good
    return ok


def _route_kernel(qt_ref, sub_ref, g_ref, thr_ref, cf_ref, k1_ref, e1_ref):
    t = qt_ref.shape[1]
    lane_tiles = [slice(c * LANES, (c + 1) * LANES) for c in range(t // LANES)]
    refs = (qt_ref, sub_ref, thr_ref, cf_ref, k1_ref, e1_ref, lane_tiles)

    for h in range(PEER_HEADS):
        ok = _route_head(h, *refs, exact=False)
        all_distinct = jnp.min(jnp.where(ok, 1.0, 0.0)) > 0.0

        @pl.when(jnp.logical_not(all_distinct))
        def _redo_with_tie_breaks():
            _route_head(h, *refs, exact=True)

    zero = jnp.zeros((), BF16)

    def expert_rows(i1, _):
        gate = None
        for h in range(PEER_HEADS):
            thr = thr_ref[h, pl.ds(i1, 1), :].astype(BF16)
            cf = cf_ref[h, pl.ds(i1, 1), :].astype(BF16)
            g = cf * jnp.where(k1_ref[h] >= thr, e1_ref[h], zero)
            gate = g if gate is None else gate + g
        rows = pl.ds(pl.multiple_of(i1 * PEER_NKEYS, PEER_NKEYS), PEER_NKEYS)
        for c, lt in enumerate(lane_tiles):
            g_ref[c, rows, :] = gate[:, lt]
        return 0

    lax.fori_loop(0, PEER_NKEYS, expert_rows, 0)


def _peer_route(qt, subkeys, *, t=256):
    Q, T = qt.shape
    t = _tile(T, t)
    E = PEER_NKEYS * PEER_NKEYS
    scratch = pltpu.VMEM((PEER_HEADS, PEER_NKEYS, t), F32)
    scratch_bf = pltpu.VMEM((PEER_HEADS, PEER_NKEYS, t), BF16)
    return pl.pallas_call(
        _route_kernel,
        out_shape=jax.ShapeDtypeStruct((T // LANES, E, LANES), BF16),
        grid=(T // t,),
        in_specs=[pl.BlockSpec((Q, t), lambda i: (0, i)),
                  pl.BlockSpec(subkeys.shape, lambda i: (0, 0, 0, 0))],
        out_specs=pl.BlockSpec((t // LANES, E, LANES), lambda i: (i, 0, 0)),
        scratch_shapes=[scratch, scratch, scratch_bf, scratch_bf],
        compiler_params=_cparams("parallel"),
        name="peer_route",
    )(qt, subkeys)


def _peer_kernel(hn_ref, u_ref, v_ref, g_ref, h_ref, o_ref):
    e = pl.program_id(1)

    @pl.when(e == 0)
    def _residual():
        o_ref[...] = h_ref[...]

    a = jnp.dot(u_ref[...], hn_ref[...], preferred_element_type=F32)
    cols = []
    for c in range(g_ref.shape[0]):
        ac = a[:, c * LANES:(c + 1) * LANES]
        gelu = 0.5 * ac * (1.0 + lax.erf(ac * (2.0 ** -0.5)))
        cols.append((gelu * g_ref[c].astype(F32)).astype(BF16))
    act = jnp.concatenate(cols, axis=1) if len(cols) > 1 else cols[0]
    o_ref[...] += lax.dot_general(act, v_ref[...], (((0,), (0,)), ((), ())),
                                  preferred_element_type=F32)


def _peer_experts(hn_t, u_bf, v_bf, g_tiles, h, *, tm=512, te=512):
    D, T = hn_t.shape
    E = u_bf.shape[0]
    tm, te = _tile(T, tm), _tile(E, te)
    return pl.pallas_call(
        _peer_kernel,
        out_shape=jax.ShapeDtypeStruct((T, D), F32),
        grid=(T // tm, E // te),
        in_specs=[pl.BlockSpec((D, tm), lambda i, e: (0, i)),
                  pl.BlockSpec((te, D), lambda i, e: (e, 0)),
                  pl.BlockSpec((te, D), lambda i, e: (e, 0)),
                  pl.BlockSpec((tm // LANES, te, LANES), lambda i, e: (i, e, 0)),
                  pl.BlockSpec((tm, D), lambda i, e: (i, 0), pipeline_mode=pl.Buffered(1))],
        out_specs=pl.BlockSpec((tm, D), lambda i, e: (i, 0)),
        compiler_params=_cparams("parallel", "arbitrary"),
        name="peer_experts",
    )(hn_t, u_bf, v_bf, g_tiles, h)


def _rope_tables(S):
    half = HEAD_DIM // 2
    inv = ROPE_THETA ** (-jnp.arange(half, dtype=F32) / half)
    ang = jnp.arange(S).astype(F32)[:, None] * inv[None, :]
    cos, sin = jnp.cos(ang), jnp.sin(ang)
    return jnp.concatenate([cos, cos], axis=-1), jnp.concatenate([-sin, sin], axis=-1)


def kernel(x, norm_mix, w_in, pool_w, pool_scale, q_norm, k_norm, w_pool_out,
           w_attn_out, w_o, norm_ffn, peer_wq, peer_subkeys, peer_u, peer_v):
    B, S, D = x.shape
    T = B * S
    depth = w_in.shape[0]
    pool_width = POOL_GROUPS * pool_w.shape[-1]
    attn_width = w_attn_out.shape[1]
    n_heads = attn_width // HEAD_DIM
    in_width = w_in.shape[-1]
    assert in_width == pool_width + 3 * attn_width + 2 * D
    assert S % MOBA_BLOCK == 0 and T % LANES == 0
    q_col = pool_width // HEAD_DIM
    k_col = q_col + n_heads
    v_col = k_col + n_heads
    gp_col = pool_width + 3 * attn_width
    ga_col = gp_col + D
    cos, sin_signed = _rope_tables(S)

    h = x.reshape(T, D)
    for l in range(depth):
        xn = _rmsnorm_bf16(h, norm_mix[l], transpose=False)
        z = _matmul(xn, w_in[l].astype(BF16), name="in_proj")
        z3 = z.reshape(B, S, in_width)
        y_pool = _pool_mixer(z3, pool_w[l].astype(BF16), pool_scale[l])
        y_attn = _moba_attention(z3, cos, sin_signed, q_norm[l], k_norm[l], n_heads=n_heads,
                                 q_col=q_col, k_col=k_col, v_col=v_col).reshape(T, attn_width)
        merged = _merge(y_pool, y_attn, w_pool_out[l].astype(BF16), w_attn_out[l].astype(BF16), z,
                        gp_col=gp_col, ga_col=ga_col)
        h = _proj_residual(merged, w_o[l].astype(BF16), h)

        hn_t = _rmsnorm_bf16(h, norm_ffn[l], transpose=True)
        qt = _matmul(peer_wq[l].T.astype(BF16), hn_t, name="peer_query")
        g_tiles = _peer_route(qt, peer_subkeys[l])
        h = _peer_experts(hn_t, peer_u[l].astype(BF16), peer_v[l].astype(BF16), g_tiles, h)
    return h.reshape(B, S, D)
```

```python
import functools

import jax
import jax.numpy as jnp
from jax import lax
from jax.experimental import pallas as pl
from jax.experimental.pallas import tpu as pltpu

POOL_WINDOWS = (2, 4, 8, 16)
POOL_GROUPS = len(POOL_WINDOWS)
POOL_HALO = 16
HEAD_DIM = 128
MOBA_BLOCK = 256
MOBA_TOPK = 3
MOBA_CHUNK = 4
MOBA_HEADS_PER_STEP = 2
ROPE_THETA = 10000.0
PEER_HEADS = 8
PEER_NKEYS = 128
PEER_HALF = 128
PEER_TOPK = 16
EPS = 1e-6
NEG = -1e30
LOG2_E = 1.4426950408889634

LANES = 128
SUBLANES = 8
VMEM_LIMIT_BYTES = 56 * 1024 * 1024

F32 = jnp.float32
BF16 = jnp.bfloat16


def _cparams(*sem):
    return pltpu.CompilerParams(dimension_semantics=sem, vmem_limit_bytes=VMEM_LIMIT_BYTES)


def _tile(n, want):
    t = min(n, want)
    assert n % t == 0, (n, t)
    return t


def _rms_kernel(x_ref, g_ref, o_ref, *, transpose):
    x = x_ref[...]
    r = lax.rsqrt(jnp.mean(x * x, axis=-1, keepdims=True) + EPS)
    y = x * r * g_ref[...]
    if transpose:
        y = y.T
    o_ref[...] = y.astype(o_ref.dtype)


def _rmsnorm_bf16(x2d, g, *, transpose, tm=256):
    T, D = x2d.shape
    tm = _tile(T, tm)
    if transpose:
        out_shape = jax.ShapeDtypeStruct((D, T), BF16)
        out_spec = pl.BlockSpec((D, tm), lambda i: (0, i))
    else:
        out_shape = jax.ShapeDtypeStruct((T, D), BF16)
        out_spec = pl.BlockSpec((tm, D), lambda i: (i, 0))
    return pl.pallas_call(
        functools.partial(_rms_kernel, transpose=transpose),
        out_shape=out_shape,
        grid=(T // tm,),
        in_specs=[pl.BlockSpec((tm, D), lambda i: (i, 0)),
                  pl.BlockSpec((1, D), lambda i: (0, 0))],
        out_specs=out_spec,
        compiler_params=_cparams("parallel"),
        name="rmsnorm_cast",
    )(x2d, g.reshape(1, D))


def _mm_kernel(a_ref, b_ref, o_ref):
    o_ref[...] = jnp.dot(a_ref[...], b_ref[...], preferred_element_type=F32).astype(o_ref.dtype)


def _matmul(a, b, *, tm=1024, tn=1024, out_dtype=F32, name="matmul"):
    M, K = a.shape
    _, N = b.shape
    tm, tn = _tile(M, tm), _tile(N, tn)
    return pl.pallas_call(
        _mm_kernel,
        out_shape=jax.ShapeDtypeStruct((M, N), out_dtype),
        grid=(M // tm, N // tn),
        in_specs=[pl.BlockSpec((tm, K), lambda i, j: (i, 0)),
                  pl.BlockSpec((K, tn), lambda i, j: (0, j))],
        out_specs=pl.BlockSpec((tm, tn), lambda i, j: (i, j)),
        compiler_params=_cparams("parallel", "arbitrary"),
        name=name,
    )(a, b)


def _pool_kernel(x_ref, halo_ref, pw_ref, ps_ref, o_ref, *, ts, C):
    i = pl.program_id(1)
    pos = i * ts + lax.broadcasted_iota(jnp.int32, (ts, 1), 0)
    for g, w in enumerate(POOL_WINDOWS):
        x = x_ref[0, :, g * C:(g + 1) * C]
        halo = halo_ref[0, :, g * C:(g + 1) * C]
        halo = jnp.where(i == 0, 0.0, halo)
        s = jnp.concatenate([halo, x], axis=0)
        d = 1
        while d < w:
            s = s + pltpu.roll(s, d, axis=0)
            d *= 2
        s = s[POOL_HALO:]
        cnt = jnp.minimum(pos + 1, w).astype(F32)
        y = (s / cnt - x).astype(BF16)
        o = jnp.dot(y, pw_ref[g], preferred_element_type=F32) * ps_ref[:, g * C:(g + 1) * C]
        o_ref[:, g * C:(g + 1) * C] = o.astype(o_ref.dtype)


def _pool_mixer(z3, pool_w_bf, pool_scale, *, ts=512):
    B, S, _ = z3.shape
    G, C, _ = pool_w_bf.shape
    W = G * C
    ts = _tile(S, ts)
    hb = ts // POOL_HALO
    n_s = S // ts
    return pl.pallas_call(
        functools.partial(_pool_kernel, ts=ts, C=C),
        out_shape=jax.ShapeDtypeStruct((B * S, W), BF16),
        grid=(B, S // ts),
        in_specs=[pl.BlockSpec((1, ts, W), lambda b, i: (b, i, 0)),
                  pl.BlockSpec((1, POOL_HALO, W), lambda b, i: (b, jnp.maximum(i * hb - 1, 0), 0)),
                  pl.BlockSpec((G, C, C), lambda b, i: (0, 0, 0)),
                  pl.BlockSpec((1, W), lambda b, i: (0, 0))],
        out_specs=pl.BlockSpec((ts, W), lambda b, i: (b * n_s + i, 0)),
        compiler_params=_cparams("parallel", "parallel"),
        name="pool_mixer",
    )(z3, z3, pool_w_bf, pool_scale.reshape(1, W))


def _norm_rope(t, gain, cos, sin_signed):
    r = lax.rsqrt(jnp.mean(t * t, axis=-1, keepdims=True) + EPS)
    t = t * r * gain
    return t * cos + pltpu.roll(t, HEAD_DIM // 2, axis=1) * sin_signed


def _moba_kernel(zq_ref, zk_ref, zv_ref, cos_ref, sin_ref, qg_ref, kg_ref, o_ref,
                 kr_ref, vt_ref, km_ref, bias_ref, s_ref, *, nb, ch, hps):
    L = MOBA_BLOCK
    hd = HEAD_DIM
    i = pl.program_id(1)
    heads = [slice(hh * hd, (hh + 1) * hd) for hh in range(hps)]

    @pl.when(i == 0)
    def _prepare_keys_values():
        for hh, hs in enumerate(heads):
            for j in range(nb):
                rows = slice(j * L, (j + 1) * L)
                part = slice((j % ch) * L, (j % ch + 1) * L)
                k = _norm_rope(zk_ref[0, rows, hs], kg_ref[...], cos_ref[rows, :], sin_ref[rows, :])
                kr_ref[hh, j // ch, part, :] = k.astype(BF16)
                km_ref[hh, j:j + 1, :] = jnp.mean(k, axis=0, keepdims=True)
                vt_ref[hh, j // ch, :, part] = zv_ref[0, rows, hs].T.astype(BF16)

    q0 = pl.multiple_of(i * L, L)
    cos_q, sin_q = cos_ref[pl.ds(q0, L), :], sin_ref[pl.ds(q0, L), :]
    blk = lax.broadcasted_iota(jnp.int32, (nb, L), 0)
    qsb = []
    for hh, hs in enumerate(heads):
        qt = _norm_rope(zq_ref[0, :, hs], qg_ref[...], cos_q, sin_q).T
        qsb.append((qt * (hd ** -0.5 * LOG2_E)).astype(BF16))
        gate = jnp.dot(km_ref[hh], qt, preferred_element_type=F32,
                       precision=lax.Precision.HIGHEST)
        gate = jnp.where(blk < i, gate, NEG)
        bias = jnp.full((nb, L), NEG, F32)
        for _ in range(min(MOBA_TOPK, nb)):
            m = jnp.max(gate, axis=0, keepdims=True)
            idx = jnp.min(jnp.where(gate == m, blk, nb), axis=0, keepdims=True)
            hit = blk == idx
            bias = jnp.where(hit & (idx < i), 0.0, bias)
            gate = jnp.where(hit, -jnp.inf, gate)
        bias_ref[hh] = bias

    kpos = lax.broadcasted_iota(jnp.int32, (L, L), 0)
    qpos = lax.broadcasted_iota(jnp.int32, (L, L), 1)
    causal_bias = jnp.where(kpos <= qpos, 0.0, NEG)

    def fold(x, op):
        parts = [x[r * SUBLANES:(r + 1) * SUBLANES] for r in range(L // SUBLANES)]
        while len(parts) > 1:
            parts = [op(parts[k], parts[k + 1]) for k in range(0, len(parts), 2)]
        return parts[0]

    n_chunks = i // ch + 1

    def score_chunk(c, mx):
        out = []
        for hh in range(hps):
            s = jnp.dot(kr_ref[hh, c], qsb[hh], preferred_element_type=F32)
            m = mx[hh]
            for b in range(ch):
                j = c * ch + b
                bj = jnp.where(j == i, causal_bias, bias_ref[hh, pl.ds(j, 1), :])
                sb = s[b * L:(b + 1) * L] + bj
                s_ref[hh, pl.ds(pl.multiple_of(j * L, L), L), :] = sb
                m = jnp.maximum(m, fold(sb, jnp.maximum))
            out.append(m)
        return tuple(out)

    mx = lax.fori_loop(0, n_chunks, score_chunk,
                       tuple(jnp.full((SUBLANES, L), NEG, F32) for _ in range(hps)))
    mq = [jnp.max(m, axis=0, keepdims=True) for m in mx]

    def value_chunk(c, carry):
        out = []
        for hh in range(hps):
            ps, acc = carry[hh]
            for b in range(ch):
                j = c * ch + b
                p = jnp.exp2(s_ref[hh, pl.ds(pl.multiple_of(j * L, L), L), :] - mq[hh])
                ps = ps + fold(p, jnp.add)
                acc = acc + jnp.dot(vt_ref[hh, c, :, b * L:(b + 1) * L], p.astype(BF16),
                                    preferred_element_type=F32)
            out.append((ps, acc))
        return tuple(out)

    init = tuple((jnp.zeros((SUBLANES, L), F32), jnp.zeros((hd, L), F32)) for _ in range(hps))
    res = lax.fori_loop(0, n_chunks, value_chunk, init)
    for hs, (ps, acc) in zip(heads, res):
        o_ref[0, :, hs] = (acc / jnp.sum(ps, axis=0, keepdims=True)).T.astype(o_ref.dtype)


def _moba_attention(z3, cos, sin_signed, q_gain, k_gain, *, n_heads, q_col, k_col, v_col,
                    chunk=MOBA_CHUNK, hps=MOBA_HEADS_PER_STEP):
    B, S, _ = z3.shape
    L = MOBA_BLOCK
    nb = S // L
    hd = HEAD_DIM
    ch = chunk
    while nb % ch:
        ch //= 2
    assert n_heads % hps == 0 and q_col % hps == 0 and k_col % hps == 0 and v_col % hps == 0
    g = n_heads // hps
    w = hps * hd
    return pl.pallas_call(
        functools.partial(_moba_kernel, nb=nb, ch=ch, hps=hps),
        out_shape=jax.ShapeDtypeStruct((B, S, n_heads * hd), BF16),
        grid=(B * g, nb),
        in_specs=[pl.BlockSpec((1, L, w), lambda bg, i: (bg // g, i, q_col // hps + bg % g)),
                  pl.BlockSpec((1, S, w), lambda bg, i: (bg // g, 0, k_col // hps + bg % g)),
                  pl.BlockSpec((1, S, w), lambda bg, i: (bg // g, 0, v_col // hps + bg % g)),
                  pl.BlockSpec((S, hd), lambda bg, i: (0, 0)),
                  pl.BlockSpec((S, hd), lambda bg, i: (0, 0)),
                  pl.BlockSpec((1, hd), lambda bg, i: (0, 0)),
                  pl.BlockSpec((1, hd), lambda bg, i: (0, 0))],
        out_specs=pl.BlockSpec((1, L, w), lambda bg, i: (bg // g, i, bg % g)),
        scratch_shapes=[pltpu.VMEM((hps, nb // ch, ch * L, hd), BF16),
                        pltpu.VMEM((hps, nb // ch, hd, ch * L), BF16),
                        pltpu.VMEM((hps, nb, hd), F32),
                        pltpu.VMEM((hps, nb, L), F32),
                        pltpu.VMEM((hps, S, L), F32)],
        compiler_params=_cparams("parallel", "arbitrary"),
        name="moba_attention",
    )(z3, z3, z3, cos, sin_signed, q_gain.reshape(1, hd), k_gain.reshape(1, hd))


def _merge_kernel(yp_ref, ya_ref, wp_ref, wa_ref, gp_ref, ga_ref, o_ref):
    p = jnp.dot(yp_ref[...], wp_ref[...], preferred_element_type=F32)
    a = jnp.dot(ya_ref[...], wa_ref[...], preferred_element_type=F32)
    o_ref[...] = (jax.nn.sigmoid(gp_ref[...]) * p + jax.nn.sigmoid(ga_ref[...]) * a).astype(o_ref.dtype)


def _merge(y_pool, y_attn, wp, wa, z2, *, gp_col, ga_col, tm=512, tn=1024):
    T, K = y_pool.shape
    D = wp.shape[1]
    tm, tn = _tile(T, tm), _tile(D, tn)
    gp_blk, ga_blk = gp_col // tn, ga_col // tn
    return pl.pallas_call(
        _merge_kernel,
        out_shape=jax.ShapeDtypeStruct((T, D), BF16),
        grid=(T // tm, D // tn),
        in_specs=[pl.BlockSpec((tm, K), lambda i, j: (i, 0)),
                  pl.BlockSpec((tm, K), lambda i, j: (i, 0)),
                  pl.BlockSpec((K, tn), lambda i, j: (0, j)),
                  pl.BlockSpec((K, tn), lambda i, j: (0, j)),
                  pl.BlockSpec((tm, tn), lambda i, j: (i, gp_blk + j)),
                  pl.BlockSpec((tm, tn), lambda i, j: (i, ga_blk + j))],
        out_specs=pl.BlockSpec((tm, tn), lambda i, j: (i, j)),
        compiler_params=_cparams("parallel", "arbitrary"),
        name="gated_merge",
    )(y_pool, y_attn, wp, wa, z2, z2)


def _proj_res_kernel(m_ref, w_ref, x_ref, o_ref):
    o_ref[...] = x_ref[...] + jnp.dot(m_ref[...], w_ref[...], preferred_element_type=F32)


def _proj_residual(m, w, x2d, *, tm=1024, tn=1024):
    T, K = m.shape
    D = w.shape[1]
    tm, tn = _tile(T, tm), _tile(D, tn)
    return pl.pallas_call(
        _proj_res_kernel,
        out_shape=jax.ShapeDtypeStruct((T, D), F32),
        grid=(T // tm, D // tn),
        in_specs=[pl.BlockSpec((tm, K), lambda i, j: (i, 0)),
                  pl.BlockSpec((K, tn), lambda i, j: (0, j)),
                  pl.BlockSpec((tm, tn), lambda i, j: (i, j))],
        out_specs=pl.BlockSpec((tm, tn), lambda i, j: (i, j)),
        compiler_params=_cparams("parallel", "arbitrary"),
        name="out_proj_residual",
    )(m, w, x2d)


def _top16_rows(scores):
    n, t = scores[0].shape
    rows = lax.broadcasted_iota(jnp.int32, (n, t), 0).astype(F32)
    krow = lax.broadcasted_iota(jnp.int32, (PEER_TOPK, t), 0)

    def body(k, carry):
        out = []
        for work, sv, si in carry:
            m = jnp.max(work, axis=0, keepdims=True)
            idx = jnp.min(jnp.where(work == m, rows, float(n)), axis=0, keepdims=True)
            out.append((jnp.where(rows == idx, -jnp.inf, work),
                        jnp.where(krow == k, m, sv),
                        jnp.where(krow == k, idx, si)))
        return tuple(out)

    zeros = jnp.zeros((PEER_TOPK, t), F32)
    res = lax.fori_loop(0, PEER_TOPK, body, tuple((s, zeros, zeros) for s in scores))
    return [(sv, si) for _, sv, si in res]


def _top16_pairs(sorted_pairs):
    K = PEER_TOPK
    t = sorted_pairs[0][0].shape[1]
    k2 = lax.broadcasted_iota(jnp.int32, (K, t), 0).astype(F32)
    flat = (k2,) + tuple(k2[:K // 2] + float(k1 * K) for k1 in range(1, K))
    cmax = tuple(sv0[0:1, :] + sv1[0:1, :] for sv0, sv1 in sorted_pairs)

    def body(_, carry):
        out = []
        for (cand, cnt, z), cm in zip(carry, cmax):
            m = jnp.max(cand[0], axis=0, keepdims=True)
            for c in cand[1:]:
                m = jnp.maximum(m, jnp.max(c, axis=0, keepdims=True))
            idx = None
            for c, f in zip(cand, flat):
                fi = jnp.min(jnp.where(c == m, f, float(K * K)), axis=0, keepdims=True)
                idx = fi if idx is None else jnp.minimum(idx, fi)
            cand = tuple(jnp.where(f == idx, -jnp.inf, c) for c, f in zip(cand, flat))
            cnt = jnp.where(k2 == jnp.floor(idx * (1.0 / K)), cnt + 1.0, cnt)
            out.append((cand, cnt, z + jnp.exp(m - cm)))
        return tuple(out)

    init = tuple(((sv0[0:1, :] + sv1,) + tuple(sv0[k1:k1 + 1, :] + sv1[:K // 2] for k1 in range(1, K)),
                  jnp.zeros((K, t), F32), jnp.zeros((1, t), F32)) for sv0, sv1 in sorted_pairs)
    res = lax.fori_loop(0, K, body, init)
    return [(cnt, z) for _, cnt, z in res]


def _count_ge(x, bound):
    return jnp.sum(jnp.where(x >= bound, 1.0, 0.0), axis=0, keepdims=True)


def _sort_desc(xs):
    xs = list(xs)
    n = len(xs)
    size = 2
    while size <= n:
        stride = size // 2
        while stride >= 1:
            for a in range(n):
                b = a ^ stride
                if b > a:
                    hi, lo = jnp.maximum(xs[a], xs[b]), jnp.minimum(xs[a], xs[b])
                    xs[a], xs[b] = (hi, lo) if (a & size) == 0 else (lo, hi)
            stride //= 2
        size *= 2
    return xs


def _top16_values(s):
    K = PEER_TOPK
    lists = _sort_desc([s[g * SUBLANES:(g + 1) * SUBLANES] for g in range(s.shape[0] // SUBLANES)])
    assert len(lists) == K
    shift = SUBLANES // 2
    while shift >= 1:
        other = [pltpu.roll(x, shift, axis=0) for x in lists]
        merged = [jnp.maximum(lists[k], other[K - 1 - k]) for k in range(K)]
        stride = K // 2
        while stride >= 1:
            for a in range(K):
                b = a ^ stride
                if b > a:
                    merged[a], merged[b] = jnp.maximum(merged[a], merged[b]), jnp.minimum(merged[a], merged[b])
            stride //= 2
        lists = merged
        shift //= 2
    return jnp.concatenate([x[0:1, :] for x in lists], axis=0)


def _staircase_sums(sv0, sv1):
    K = PEER_TOPK
    t = sv0.shape[1]
    k1_all = lax.broadcasted_iota(jnp.int32, (K, t), 0)
    k1_low = k1_all[:K // 2]
    cols = [jnp.where(k1_all >= 2, sv0 + sv1[0:1, :], -jnp.inf)]
    for k2 in range(1, K // 3):
        rows_in = K // (k2 + 1)
        cols.append(jnp.where((k1_low >= 2) & (k1_low < rows_in), sv0[:K // 2] + sv1[k2:k2 + 1, :], -jnp.inf))
    return sv0[0:1, :] + sv1, sv0[1:2, :] + sv1[:K // 2], cols


def _top16_pairs_distinct(sorted_pairs):
    K = PEER_TOPK
    t = sorted_pairs[0][0].shape[1]
    k1_low = lax.broadcasted_iota(jnp.int32, (K // 2, t), 0)
    cmax = tuple(sv0[0:1, :] + sv1[0:1, :] for sv0, sv1 in sorted_pairs)

    def flatten(stairs):
        row0, row1, cols = stairs
        return (row0, row1) + tuple(cols)

    def body(_, carry):
        out = []
        for (cand, _, z), cm in zip(carry, cmax):
            m = jnp.max(cand[0], axis=0, keepdims=True)
            for c in cand[1:]:
                m = jnp.maximum(m, jnp.max(c, axis=0, keepdims=True))
            out.append((tuple(jnp.where(c == m, -jnp.inf, c) for c in cand), m, z + jnp.exp(m - cm)))
        return tuple(out)

    zero = jnp.zeros((1, t), F32)
    stairs = [_staircase_sums(sv0, sv1) for sv0, sv1 in sorted_pairs]
    res = lax.fori_loop(0, K, body, tuple((flatten(s), zero, zero) for s in stairs))
    out = []
    for (row0, row1, cols), (_, kth, z) in zip(stairs, res):
        ge = [jnp.where(c >= kth, 1.0, 0.0) for c in cols]
        low = ge[0][:K // 2]
        for g in ge[1:]:
            low = low + g
        low = jnp.where(k1_low == 0, _count_ge(row0, kth), jnp.where(k1_low == 1, _count_ge(row1, kth), low))
        cnt = jnp.concatenate([low, ge[0][K // 2:]], axis=0)
        distinct = jnp.sum(cnt, axis=0, keepdims=True) == float(K)
        out.append((cnt, z, distinct))
    return out


def _route_head(h, qt_ref, sub_ref, thr_ref, cf_ref, k1_ref, e1_ref, lane_tiles, exact):
    hi = lax.Precision.HIGHEST
    K = PEER_TOPK
    base = h * 2 * PEER_HALF
    scores = []
    for lt in lane_tiles:
        for half in range(2):
            rows = slice(base + half * PEER_HALF, base + (half + 1) * PEER_HALF)
            scores.append(jnp.dot(sub_ref[h, half], qt_ref[rows, lt],
                                  preferred_element_type=F32, precision=hi))
    n = len(lane_tiles)
    if exact:
        tops = _top16_rows(scores)
        svs = [sv for sv, _ in tops]
        stairs = [(cnt, z, None) for cnt, z in _top16_pairs([(svs[2 * c], svs[2 * c + 1]) for c in range(n)])]
        key_row = lax.broadcasted_iota(jnp.int32, (PEER_NKEYS, LANES), 0).astype(F32)
        match = [[key_row == si[k:k + 1, :] for k in range(K)] for _, si in tops]
    else:
        svs = [_top16_values(s) for s in scores]
        stairs = _top16_pairs_distinct([(svs[2 * c], svs[2 * c + 1]) for c in range(n)])
        match = [[s == sv[k:k + 1, :] for k in range(K)] for s, sv in zip(scores, svs)]
    ok = None
    for c, lt in enumerate(lane_tiles):
        s0, s1 = scores[2 * c], scores[2 * c + 1]
        sv0, sv1 = svs[2 * c], svs[2 * c + 1]
        cnt, z, distinct = stairs[c]
        paired = jnp.zeros_like(s0)
        key1 = jnp.full_like(s1, -float(K))
        for k in range(K):
            paired = jnp.where(match[2 * c][k], cnt[k:k + 1, :], paired)
            key1 = jnp.where(match[2 * c + 1][k], -float(k), key1)
        thr_ref[h, :, lt] = 1.0 - paired
        cf_ref[h, :, lt] = jnp.exp(s0 - sv0[0:1, :]) / z
        e1_ref[h, :, lt] = jnp.exp(s1 - sv1[0:1, :]).astype(e1_ref.dtype)
        k1_ref[h, :, lt] = key1.astype(k1_ref.dtype)
        if not exact:
            good = (distinct & (_count_ge(s0, sv0[K - 1:K, :]) == float(K))
                    & (_count_ge(s1, sv1[K - 1:K, :]) == float(K)))
            ok = good if ok is None else ok & good
    return ok


def _route_kernel(qt_ref, sub_ref, g_ref, thr_ref, cf_ref, k1_ref, e1_ref):
    t = qt_ref.shape[1]
    lane_tiles = [slice(c * LANES, (c + 1) * LANES) for c in range(t // LANES)]
    refs = (qt_ref, sub_ref, thr_ref, cf_ref, k1_ref, e1_ref, lane_tiles)

    for h in range(PEER_HEADS):
        ok = _route_head(h, *refs, exact=False)
        all_distinct = jnp.min(jnp.where(ok, 1.0, 0.0)) > 0.0

        @pl.when(jnp.logical_not(all_distinct))
        def _redo_with_tie_breaks():
            _route_head(h, *refs, exact=True)

    zero = jnp.zeros((), BF16)

    def expert_rows(i1, _):
        gate = None
        for h in range(PEER_HEADS):
            thr = thr_ref[h, pl.ds(i1, 1), :].astype(BF16)
            cf = cf_ref[h, pl.ds(i1, 1), :].astype(BF16)
            g = cf * jnp.where(k1_ref[h] >= thr, e1_ref[h], zero)
            gate = g if gate is None else gate + g
        rows = pl.ds(pl.multiple_of(i1 * PEER_NKEYS, PEER_NKEYS), PEER_NKEYS)
        for c, lt in enumerate(lane_tiles):
            g_ref[c, rows, :] = gate[:, lt]
        return 0

    lax.fori_loop(0, PEER_NKEYS, expert_rows, 0)


def _peer_route(qt, subkeys, *, t=256):
    Q, T = qt.shape
    t = _tile(T, t)
    E = PEER_NKEYS * PEER_NKEYS
    scratch = pltpu.VMEM((PEER_HEADS, PEER_NKEYS, t), F32)
    scratch_bf = pltpu.VMEM((PEER_HEADS, PEER_NKEYS, t), BF16)
    return pl.pallas_call(
        _route_kernel,
        out_shape=jax.ShapeDtypeStruct((T // LANES, E, LANES), BF16),
        grid=(T // t,),
        in_specs=[pl.BlockSpec((Q, t), lambda i: (0, i)),
                  pl.BlockSpec(subkeys.shape, lambda i: (0, 0, 0, 0))],
        out_specs=pl.BlockSpec((t // LANES, E, LANES), lambda i: (i, 0, 0)),
        scratch_shapes=[scratch, scratch, scratch_bf, scratch_bf],
        compiler_params=_cparams("parallel"),
        name="peer_route",
    )(qt, subkeys)


def _peer_kernel(hn_ref, u_ref, v_ref, g_ref, h_ref, o_ref):
    e = pl.program_id(1)

    @pl.when(e == 0)
    def _residual():
        o_ref[...] = h_ref[...]

    a = jnp.dot(u_ref[...], hn_ref[...], preferred_element_type=F32)
    cols = []
    for c in range(g_ref.shape[0]):
        ac = a[:, c * LANES:(c + 1) * LANES]
        gelu = 0.5 * ac * (1.0 + lax.erf(ac * (2.0 ** -0.5)))
        cols.append((gelu * g_ref[c].astype(F32)).astype(BF16))
    act = jnp.concatenate(cols, axis=1) if len(cols) > 1 else cols[0]
    o_ref[...] += lax.dot_general(act, v_ref[...], (((0,), (0,)), ((), ())),
                                  preferred_element_type=F32)


def _peer_experts(hn_t, u_bf, v_bf, g_tiles, h, *, tm=512, te=1024):
    D, T = hn_t.shape
    E = u_bf.shape[0]
    tm, te = _tile(T, tm), _tile(E, te)
    once = pl.Buffered(1)
    return pl.pallas_call(
        _peer_kernel,
        out_shape=jax.ShapeDtypeStruct((T, D), F32),
        grid=(T // tm, E // te),
        in_specs=[pl.BlockSpec((D, tm), lambda i, e: (0, i), pipeline_mode=once),
                  pl.BlockSpec((te, D), lambda i, e: (e, 0)),
                  pl.BlockSpec((te, D), lambda i, e: (e, 0)),
                  pl.BlockSpec((tm // LANES, te, LANES), lambda i, e: (i, e, 0)),
                  pl.BlockSpec((tm, D), lambda i, e: (i, 0), pipeline_mode=once)],
        out_specs=pl.BlockSpec((tm, D), lambda i, e: (i, 0), pipeline_mode=once),
        compiler_params=_cparams("parallel", "arbitrary"),
        name="peer_experts",
    )(hn_t, u_bf, v_bf, g_tiles, h)


def _rope_tables(S):
    half = HEAD_DIM // 2
    inv = ROPE_THETA ** (-jnp.arange(half, dtype=F32) / half)
    ang = jnp.arange(S).astype(F32)[:, None] * inv[None, :]
    cos, sin = jnp.cos(ang), jnp.sin(ang)
    return jnp.concatenate([cos, cos], axis=-1), jnp.concatenate([-sin, sin], axis=-1)


def kernel(x, norm_mix, w_in, pool_w, pool_scale, q_norm, k_norm, w_pool_out,
           w_attn_out, w_o, norm_ffn, peer_wq, peer_subkeys, peer_u, peer_v):
    B, S, D = x.shape
    T = B * S
    depth = w_in.shape[0]
    pool_width = POOL_GROUPS * pool_w.shape[-1]
    attn_width = w_attn_out.shape[1]
    n_heads = attn_width // HEAD_DIM
    in_width = w_in.shape[-1]
    assert in_width == pool_width + 3 * attn_width + 2 * D
    assert S % MOBA_BLOCK == 0 and T % LANES == 0
    q_col = pool_width // HEAD_DIM
    k_col = q_col + n_heads
    v_col = k_col + n_heads
    gp_col = pool_width + 3 * attn_width
    ga_col = gp_col + D
    cos, sin_signed = _rope_tables(S)

    h = x.reshape(T, D)
    for l in range(depth):
        xn = _rmsnorm_bf16(h, norm_mix[l], transpose=False)
        z = _matmul(xn, w_in[l].astype(BF16), name="in_proj")
        z3 = z.reshape(B, S, in_width)
        y_pool = _pool_mixer(z3, pool_w[l].astype(BF16), pool_scale[l])
        y_attn = _moba_attention(z3, cos, sin_signed, q_norm[l], k_norm[l], n_heads=n_heads,
                                 q_col=q_col, k_col=k_col, v_col=v_col).reshape(T, attn_width)
        merged = _merge(y_pool, y_attn, w_pool_out[l].astype(BF16), w_attn_out[l].astype(BF16), z,
                        gp_col=gp_col, ga_col=ga_col)
        h = _proj_residual(merged, w_o[l].astype(BF16), h)

        hn_t = _rmsnorm_bf16(h, norm_ffn[l], transpose=True)
        qt = _matmul(peer_wq[l].T.astype(BF16), hn_t, name="peer_query")
        g_tiles = _peer_route(qt, peer_subkeys[l])
        h = _peer_experts(hn_t, peer_u[l].astype(BF16), peer_v[l].astype(BF16), g_tiles, h)
    return h.reshape(B, S, D)
```

```python
import functools

import jax
import jax.numpy as jnp
from jax import lax
from jax.experimental import pallas as pl
from jax.experimental.pallas import tpu as pltpu

POOL_WINDOWS = (2, 4, 8, 16)
POOL_GROUPS = len(POOL_WINDOWS)
POOL_HALO = 16
HEAD_DIM = 128
MOBA_BLOCK = 256
MOBA_TOPK = 3
MOBA_CHUNK = 4
MOBA_HEADS_PER_STEP = 2
ROPE_THETA = 10000.0
PEER_HEADS = 8
PEER_NKEYS = 128
PEER_HALF = 128
PEER_TOPK = 16
EPS = 1e-6
NEG = -1e30
LOG2_E = 1.4426950408889634

LANES = 128
SUBLANES = 8
VMEM_LIMIT_BYTES = 56 * 1024 * 1024

F32 = jnp.float32
BF16 = jnp.bfloat16


def _cparams(*sem):
    return pltpu.CompilerParams(dimension_semantics=sem, vmem_limit_bytes=VMEM_LIMIT_BYTES)


def _tile(n, want):
    t = min(n, want)
    assert n % t == 0, (n, t)
    return t


def _rms_kernel(x_ref, g_ref, o_ref, *, transpose):
    x = x_ref[...]
    r = lax.rsqrt(jnp.mean(x * x, axis=-1, keepdims=True) + EPS)
    y = x * r * g_ref[...]
    if transpose:
        y = y.T
    o_ref[...] = y.astype(o_ref.dtype)


def _rmsnorm_bf16(x2d, g, *, transpose, tm=256):
    T, D = x2d.shape
    tm = _tile(T, tm)
    if transpose:
        out_shape = jax.ShapeDtypeStruct((D, T), BF16)
        out_spec = pl.BlockSpec((D, tm), lambda i: (0, i))
    else:
        out_shape = jax.ShapeDtypeStruct((T, D), BF16)
        out_spec = pl.BlockSpec((tm, D), lambda i: (i, 0))
    return pl.pallas_call(
        functools.partial(_rms_kernel, transpose=transpose),
        out_shape=out_shape,
        grid=(T // tm,),
        in_specs=[pl.BlockSpec((tm, D), lambda i: (i, 0)),
                  pl.BlockSpec((1, D), lambda i: (0, 0))],
        out_specs=out_spec,
        compiler_params=_cparams("parallel"),
        name="rmsnorm_cast",
    )(x2d, g.reshape(1, D))


def _mm_kernel(a_ref, b_ref, o_ref):
    o_ref[...] = jnp.dot(a_ref[...], b_ref[...], preferred_element_type=F32).astype(o_ref.dtype)


def _matmul(a, b, *, tm=1024, tn=1024, out_dtype=F32, name="matmul"):
    M, K = a.shape
    _, N = b.shape
    tm, tn = _tile(M, tm), _tile(N, tn)
    return pl.pallas_call(
        _mm_kernel,
        out_shape=jax.ShapeDtypeStruct((M, N), out_dtype),
        grid=(M // tm, N // tn),
        in_specs=[pl.BlockSpec((tm, K), lambda i, j: (i, 0)),
                  pl.BlockSpec((K, tn), lambda i, j: (0, j))],
        out_specs=pl.BlockSpec((tm, tn), lambda i, j: (i, j)),
        compiler_params=_cparams("parallel", "arbitrary"),
        name=name,
    )(a, b)


def _pool_kernel(x_ref, halo_ref, pw_ref, ps_ref, o_ref, *, ts, C):
    i = pl.program_id(1)
    pos = i * ts + lax.broadcasted_iota(jnp.int32, (ts, 1), 0)
    for g, w in enumerate(POOL_WINDOWS):
        x = x_ref[0, :, g * C:(g + 1) * C]
        halo = halo_ref[0, :, g * C:(g + 1) * C]
        halo = jnp.where(i == 0, 0.0, halo)
        s = jnp.concatenate([halo, x], axis=0)
        d = 1
        while d < w:
            s = s + pltpu.roll(s, d, axis=0)
            d *= 2
        s = s[POOL_HALO:]
        cnt = jnp.minimum(pos + 1, w).astype(F32)
        y = (s / cnt - x).astype(BF16)
        o = jnp.dot(y, pw_ref[g], preferred_element_type=F32) * ps_ref[:, g * C:(g + 1) * C]
        o_ref[:, g * C:(g + 1) * C] = o.astype(o_ref.dtype)


def _pool_mixer(z3, pool_w_bf, pool_scale, *, ts=512):
    B, S, _ = z3.shape
    G, C, _ = pool_w_bf.shape
    W = G * C
    ts = _tile(S, ts)
    hb = ts // POOL_HALO
    n_s = S // ts
    return pl.pallas_call(
        functools.partial(_pool_kernel, ts=ts, C=C),
        out_shape=jax.ShapeDtypeStruct((B * S, W), BF16),
        grid=(B, S // ts),
        in_specs=[pl.BlockSpec((1, ts, W), lambda b, i: (b, i, 0)),
                  pl.BlockSpec((1, POOL_HALO, W), lambda b, i: (b, jnp.maximum(i * hb - 1, 0), 0)),
                  pl.BlockSpec((G, C, C), lambda b, i: (0, 0, 0)),
                  pl.BlockSpec((1, W), lambda b, i: (0, 0))],
        out_specs=pl.BlockSpec((ts, W), lambda b, i: (b * n_s + i, 0)),
        compiler_params=_cparams("parallel", "parallel"),
        name="pool_mixer",
    )(z3, z3, pool_w_bf, pool_scale.reshape(1, W))


def _norm_rope(t, gain, cos, sin_signed):
    r = lax.rsqrt(jnp.mean(t * t, axis=-1, keepdims=True) + EPS)
    t = t * r * gain
    return t * cos + pltpu.roll(t, HEAD_DIM // 2, axis=1) * sin_signed


def _moba_kernel(zq_ref, zk_ref, zv_ref, cos_ref, sin_ref, qg_ref, kg_ref, o_ref,
                 kr_ref, vt_ref, km_ref, bias_ref, s_ref, *, nb, ch, hps):
    L = MOBA_BLOCK
    hd = HEAD_DIM
    i = pl.program_id(1)
    heads = [slice(hh * hd, (hh + 1) * hd) for hh in range(hps)]

    @pl.when(i == 0)
    def _prepare_keys_values():
        for hh, hs in enumerate(heads):
            for j in range(nb):
                rows = slice(j * L, (j + 1) * L)
                part = slice((j % ch) * L, (j % ch + 1) * L)
                k = _norm_rope(zk_ref[0, rows, hs], kg_ref[...], cos_ref[rows, :], sin_ref[rows, :])
                kr_ref[hh, j // ch, part, :] = k.astype(BF16)
                km_ref[hh, j:j + 1, :] = jnp.mean(k, axis=0, keepdims=True)
                vt_ref[hh, j // ch, :, part] = zv_ref[0, rows, hs].T.astype(BF16)

    q0 = pl.multiple_of(i * L, L)
    cos_q, sin_q = cos_ref[pl.ds(q0, L), :], sin_ref[pl.ds(q0, L), :]
    blk = lax.broadcasted_iota(jnp.int32, (nb, L), 0)
    qsb = []
    for hh, hs in enumerate(heads):
        qt = _norm_rope(zq_ref[0, :, hs], qg_ref[...], cos_q, sin_q).T
        qsb.append((qt * (hd ** -0.5 * LOG2_E)).astype(BF16))
        gate = jnp.dot(km_ref[hh], qt, preferred_element_type=F32,
                       precision=lax.Precision.HIGHEST)
        gate = jnp.where(blk < i, gate, NEG)
        bias = jnp.full((nb, L), NEG, F32)
        for _ in range(min(MOBA_TOPK, nb)):
            m = jnp.max(gate, axis=0, keepdims=True)
            idx = jnp.min(jnp.where(gate == m, blk, nb), axis=0, keepdims=True)
            hit = blk == idx
            bias = jnp.where(hit & (idx < i), 0.0, bias)
            gate = jnp.where(hit, -jnp.inf, gate)
        bias_ref[hh] = bias

    kpos = lax.broadcasted_iota(jnp.int32, (L, L), 0)
    qpos = lax.broadcasted_iota(jnp.int32, (L, L), 1)
    causal_bias = jnp.where(kpos <= qpos, 0.0, NEG)

    def fold(x, op):
        parts = [x[r * SUBLANES:(r + 1) * SUBLANES] for r in range(L // SUBLANES)]
        while len(parts) > 1:
            parts = [op(parts[k], parts[k + 1]) for k in range(0, len(parts), 2)]
        return parts[0]

    n_chunks = i // ch + 1

    def score_chunk(c, mx):
        out = []
        for hh in range(hps):
            s = jnp.dot(kr_ref[hh, c], qsb[hh], preferred_element_type=F32)
            m = mx[hh]
            for b in range(ch):
                j = c * ch + b
                bj = jnp.where(j == i, causal_bias, bias_ref[hh, pl.ds(j, 1), :])
                sb = s[b * L:(b + 1) * L] + bj
                s_ref[hh, pl.ds(pl.multiple_of(j * L, L), L), :] = sb
                m = jnp.maximum(m, fold(sb, jnp.maximum))
            out.append(m)
        return tuple(out)

    mx = lax.fori_loop(0, n_chunks, score_chunk,
                       tuple(jnp.full((SUBLANES, L), NEG, F32) for _ in range(hps)))
    mq = [jnp.max(m, axis=0, keepdims=True) for m in mx]

    def value_chunk(c, carry):
        out = []
        for hh in range(hps):
            ps, acc = carry[hh]
            for b in range(ch):
                j = c * ch + b
                p = jnp.exp2(s_ref[hh, pl.ds(pl.multiple_of(j * L, L), L), :] - mq[hh])
                ps = ps + fold(p, jnp.add)
                acc = acc + jnp.dot(vt_ref[hh, c, :, b * L:(b + 1) * L], p.astype(BF16),
                                    preferred_element_type=F32)
            out.append((ps, acc))
        return tuple(out)

    init = tuple((jnp.zeros((SUBLANES, L), F32), jnp.zeros((hd, L), F32)) for _ in range(hps))
    res = lax.fori_loop(0, n_chunks, value_chunk, init)
    for hs, (ps, acc) in zip(heads, res):
        o_ref[0, :, hs] = (acc / jnp.sum(ps, axis=0, keepdims=True)).T.astype(o_ref.dtype)


def _moba_attention(z3, cos, sin_signed, q_gain, k_gain, *, n_heads, q_col, k_col, v_col,
                    chunk=MOBA_CHUNK, hps=MOBA_HEADS_PER_STEP):
    B, S, _ = z3.shape
    L = MOBA_BLOCK
    nb = S // L
    hd = HEAD_DIM
    ch = chunk
    while nb % ch:
        ch //= 2
    assert n_heads % hps == 0 and q_col % hps == 0 and k_col % hps == 0 and v_col % hps == 0
    g = n_heads // hps
    w = hps * hd
    return pl.pallas_call(
        functools.partial(_moba_kernel, nb=nb, ch=ch, hps=hps),
        out_shape=jax.ShapeDtypeStruct((B, S, n_heads * hd), BF16),
        grid=(B * g, nb),
        in_specs=[pl.BlockSpec((1, L, w), lambda bg, i: (bg // g, i, q_col // hps + bg % g)),
                  pl.BlockSpec((1, S, w), lambda bg, i: (bg // g, 0, k_col // hps + bg % g)),
                  pl.BlockSpec((1, S, w), lambda bg, i: (bg // g, 0, v_col // hps + bg % g)),
                  pl.BlockSpec((S, hd), lambda bg, i: (0, 0)),
                  pl.BlockSpec((S, hd), lambda bg, i: (0, 0)),
                  pl.BlockSpec((1, hd), lambda bg, i: (0, 0)),
                  pl.BlockSpec((1, hd), lambda bg, i: (0, 0))],
        out_specs=pl.BlockSpec((1, L, w), lambda bg, i: (bg // g, i, bg % g)),
        scratch_shapes=[pltpu.VMEM((hps, nb // ch, ch * L, hd), BF16),
                        pltpu.VMEM((hps, nb // ch, hd, ch * L), BF16),
                        pltpu.VMEM((hps, nb, hd), F32),
                        pltpu.VMEM((hps, nb, L), F32),
                        pltpu.VMEM((hps, S, L), F32)],
        compiler_params=_cparams("parallel", "arbitrary"),
        name="moba_attention",
    )(z3, z3, z3, cos, sin_signed, q_gain.reshape(1, hd), k_gain.reshape(1, hd))


def _merge_kernel(yp_ref, ya_ref, wp_ref, wa_ref, gp_ref, ga_ref, o_ref):
    p = jnp.dot(yp_ref[...], wp_ref[...], preferred_element_type=F32)
    a = jnp.dot(ya_ref[...], wa_ref[...], preferred_element_type=F32)
    o_ref[...] = (jax.nn.sigmoid(gp_ref[...]) * p + jax.nn.sigmoid(ga_ref[...]) * a).astype(o_ref.dtype)


def _merge(y_pool, y_attn, wp, wa, z2, *, gp_col, ga_col, tm=512, tn=1024):
    T, K = y_pool.shape
    D = wp.shape[1]
    tm, tn = _tile(T, tm), _tile(D, tn)
    gp_blk, ga_blk = gp_col // tn, ga_col // tn
    return pl.pallas_call(
        _merge_kernel,
        out_shape=jax.ShapeDtypeStruct((T, D), BF16),
        grid=(T // tm, D // tn),
        in_specs=[pl.BlockSpec((tm, K), lambda i, j: (i, 0)),
                  pl.BlockSpec((tm, K), lambda i, j: (i, 0)),
                  pl.BlockSpec((K, tn), lambda i, j: (0, j)),
                  pl.BlockSpec((K, tn), lambda i, j: (0, j)),
                  pl.BlockSpec((tm, tn), lambda i, j: (i, gp_blk + j)),
                  pl.BlockSpec((tm, tn), lambda i, j: (i, ga_blk + j))],
        out_specs=pl.BlockSpec((tm, tn), lambda i, j: (i, j)),
        compiler_params=_cparams("parallel", "arbitrary"),
        name="gated_merge",
    )(y_pool, y_attn, wp, wa, z2, z2)


def _proj_res_kernel(m_ref, w_ref, x_ref, o_ref):
    o_ref[...] = x_ref[...] + jnp.dot(m_ref[...], w_ref[...], preferred_element_type=F32)


def _proj_residual(m, w, x2d, *, tm=1024, tn=1024):
    T, K = m.shape
    D = w.shape[1]
    tm, tn = _tile(T, tm), _tile(D, tn)
    return pl.pallas_call(
        _proj_res_kernel,
        out_shape=jax.ShapeDtypeStruct((T, D), F32),
        grid=(T // tm, D // tn),
        in_specs=[pl.BlockSpec((tm, K), lambda i, j: (i, 0)),
                  pl.BlockSpec((K, tn), lambda i, j: (0, j)),
                  pl.BlockSpec((tm, tn), lambda i, j: (i, j))],
        out_specs=pl.BlockSpec((tm, tn), lambda i, j: (i, j)),
        compiler_params=_cparams("parallel", "arbitrary"),
        name="out_proj_residual",
    )(m, w, x2d)


def _top16_rows(scores):
    n, t = scores[0].shape
    rows = lax.broadcasted_iota(jnp.int32, (n, t), 0).astype(F32)
    krow = lax.broadcasted_iota(jnp.int32, (PEER_TOPK, t), 0)

    def body(k, carry):
        out = []
        for work, sv, si in carry:
            m = jnp.max(work, axis=0, keepdims=True)
            idx = jnp.min(jnp.where(work == m, rows, float(n)), axis=0, keepdims=True)
            out.append((jnp.where(rows == idx, -jnp.inf, work),
                        jnp.where(krow == k, m, sv),
                        jnp.where(krow == k, idx, si)))
        return tuple(out)

    zeros = jnp.zeros((PEER_TOPK, t), F32)
    res = lax.fori_loop(0, PEER_TOPK, body, tuple((s, zeros, zeros) for s in scores))
    return [(sv, si) for _, sv, si in res]


def _top16_pairs(sorted_pairs):
    K = PEER_TOPK
    t = sorted_pairs[0][0].shape[1]
    k2 = lax.broadcasted_iota(jnp.int32, (K, t), 0).astype(F32)
    flat = (k2,) + tuple(k2[:K // 2] + float(k1 * K) for k1 in range(1, K))
    cmax = tuple(sv0[0:1, :] + sv1[0:1, :] for sv0, sv1 in sorted_pairs)

    def body(_, carry):
        out = []
        for (cand, cnt, z), cm in zip(carry, cmax):
            m = jnp.max(cand[0], axis=0, keepdims=True)
            for c in cand[1:]:
                m = jnp.maximum(m, jnp.max(c, axis=0, keepdims=True))
            idx = None
            for c, f in zip(cand, flat):
                fi = jnp.min(jnp.where(c == m, f, float(K * K)), axis=0, keepdims=True)
                idx = fi if idx is None else jnp.minimum(idx, fi)
            cand = tuple(jnp.where(f == idx, -jnp.inf, c) for c, f in zip(cand, flat))
            cnt = jnp.where(k2 == jnp.floor(idx * (1.0 / K)), cnt + 1.0, cnt)
            out.append((cand, cnt, z + jnp.exp(m - cm)))
        return tuple(out)

    init = tuple(((sv0[0:1, :] + sv1,) + tuple(sv0[k1:k1 + 1, :] + sv1[:K // 2] for k1 in range(1, K)),
                  jnp.zeros((K, t), F32), jnp.zeros((1, t), F32)) for sv0, sv1 in sorted_pairs)
    res = lax.fori_loop(0, K, body, init)
    return [(cnt, z) for _, cnt, z in res]


def _count_ge(x, bound):
    return jnp.sum(jnp.where(x >= bound, 1.0, 0.0), axis=0, keepdims=True)


def _sort_desc(xs):
    xs = list(xs)
    n = len(xs)
    size = 2
    while size <= n:
        stride = size // 2
        while stride >= 1:
            for a in range(n):
                b = a ^ stride
                if b > a:
                    hi, lo = jnp.maximum(xs[a], xs[b]), jnp.minimum(xs[a], xs[b])
                    xs[a], xs[b] = (hi, lo) if (a & size) == 0 else (lo, hi)
            stride //= 2
        size *= 2
    return xs


def _top16_values(s):
    K = PEER_TOPK
    lists = _sort_desc([s[g * SUBLANES:(g + 1) * SUBLANES] for g in range(s.shape[0] // SUBLANES)])
    assert len(lists) == K
    shift = SUBLANES // 2
    while shift >= 1:
        other = [pltpu.roll(x, shift, axis=0) for x in lists]
        merged = [jnp.maximum(lists[k], other[K - 1 - k]) for k in range(K)]
        stride = K // 2
        while stride >= 1:
            for a in range(K):
                b = a ^ stride
                if b > a:
                    merged[a], merged[b] = jnp.maximum(merged[a], merged[b]), jnp.minimum(merged[a], merged[b])
            stride //= 2
        lists = merged
        shift //= 2
    return jnp.concatenate([x[0:1, :] for x in lists], axis=0)


def _staircase_sums(sv0, sv1):
    K = PEER_TOPK
    t = sv0.shape[1]
    k1_all = lax.broadcasted_iota(jnp.int32, (K, t), 0)
    k1_low = k1_all[:K // 2]
    cols = [jnp.where(k1_all >= 2, sv0 + sv1[0:1, :], -jnp.inf)]
    for k2 in range(1, K // 3):
        rows_in = K // (k2 + 1)
        cols.append(jnp.where((k1_low >= 2) & (k1_low < rows_in), sv0[:K // 2] + sv1[k2:k2 + 1, :], -jnp.inf))
    return sv0[0:1, :] + sv1, sv0[1:2, :] + sv1[:K // 2], cols


def _top16_pairs_distinct(sorted_pairs):
    K = PEER_TOPK
    t = sorted_pairs[0][0].shape[1]
    k1_low = lax.broadcasted_iota(jnp.int32, (K // 2, t), 0)
    cmax = tuple(sv0[0:1, :] + sv1[0:1, :] for sv0, sv1 in sorted_pairs)

    def flatten(stairs):
        row0, row1, cols = stairs
        return (row0, row1) + tuple(cols)

    def body(_, carry):
        out = []
        for (cand, _, z), cm in zip(carry, cmax):
            m = jnp.max(cand[0], axis=0, keepdims=True)
            for c in cand[1:]:
                m = jnp.maximum(m, jnp.max(c, axis=0, keepdims=True))
            out.append((tuple(jnp.where(c == m, -jnp.inf, c) for c in cand), m, z + jnp.exp(m - cm)))
        return tuple(out)

    zero = jnp.zeros((1, t), F32)
    stairs = [_staircase_sums(sv0, sv1) for sv0, sv1 in sorted_pairs]
    res = lax.fori_loop(0, K, body, tuple((flatten(s), zero, zero) for s in stairs))
    out = []
    for (row0, row1, cols), (_, kth, z) in zip(stairs, res):
        ge = [jnp.where(c >= kth, 1.0, 0.0) for c in cols]
        low = ge[0][:K // 2]
        for g in ge[1:]:
            low = low + g
        low = jnp.where(k1_low == 0, _count_ge(row0, kth), jnp.where(k1_low == 1, _count_ge(row1, kth), low))
        cnt = jnp.concatenate([low, ge[0][K // 2:]], axis=0)
        distinct = jnp.sum(cnt, axis=0, keepdims=True) == float(K)
        out.append((cnt, z, distinct))
    return out


def _bf16_twice(x):
    bits = pltpu.bitcast(x.astype(BF16).astype(F32), jnp.uint32) >> 16
    return bits | (bits << 16)


def _route_head(h, qt_ref, sub_ref, thr_ref, cf_ref, k1_ref, e1_ref, lane_tiles, exact):
    hi = lax.Precision.HIGHEST
    K = PEER_TOPK
    base = h * 2 * PEER_HALF
    scores = []
    for lt in lane_tiles:
        for half in range(2):
            rows = slice(base + half * PEER_HALF, base + (half + 1) * PEER_HALF)
            scores.append(jnp.dot(sub_ref[h, half], qt_ref[rows, lt],
                                  preferred_element_type=F32, precision=hi))
    n = len(lane_tiles)
    if exact:
        tops = _top16_rows(scores)
        svs = [sv for sv, _ in tops]
        stairs = [(cnt, z, None) for cnt, z in _top16_pairs([(svs[2 * c], svs[2 * c + 1]) for c in range(n)])]
        key_row = lax.broadcasted_iota(jnp.int32, (PEER_NKEYS, LANES), 0).astype(F32)
        match = [[key_row == si[k:k + 1, :] for k in range(K)] for _, si in tops]
    else:
        svs = [_top16_values(s) for s in scores]
        stairs = _top16_pairs_distinct([(svs[2 * c], svs[2 * c + 1]) for c in range(n)])
        match = [[s == sv[k:k + 1, :] for k in range(K)] for s, sv in zip(scores, svs)]
    ok = None
    for c, lt in enumerate(lane_tiles):
        s0, s1 = scores[2 * c], scores[2 * c + 1]
        sv0, sv1 = svs[2 * c], svs[2 * c + 1]
        cnt, z, distinct = stairs[c]
        paired = jnp.zeros_like(s0)
        key1 = jnp.full_like(s1, -float(K))
        for k in range(K):
            paired = jnp.where(match[2 * c][k], cnt[k:k + 1, :], paired)
            key1 = jnp.where(match[2 * c + 1][k], -float(k), key1)
        thr_ref[h, :, lt] = _bf16_twice(1.0 - paired)
        cf_ref[h, :, lt] = _bf16_twice(jnp.exp(s0 - sv0[0:1, :]) / z)
        e1_ref[h, :, lt] = jnp.exp(s1 - sv1[0:1, :]).astype(e1_ref.dtype)
        k1_ref[h, :, lt] = key1.astype(k1_ref.dtype)
        if not exact:
            good = (distinct & (_count_ge(s0, sv0[K - 1:K, :]) == float(K))
                    & (_count_ge(s1, sv1[K - 1:K, :]) == float(K)))
            ok = good if ok is None else ok & good
    return ok


def _route_kernel(qt_ref, sub_ref, g_ref, thr_ref, cf_ref, k1_ref, e1_ref):
    t = qt_ref.shape[1]
    lane_tiles = [slice(c * LANES, (c + 1) * LANES) for c in range(t // LANES)]
    refs = (qt_ref, sub_ref, thr_ref, cf_ref, k1_ref, e1_ref, lane_tiles)

    for h in range(PEER_HEADS):
        ok = _route_head(h, *refs, exact=False)
        all_distinct = jnp.min(jnp.where(ok, 1.0, 0.0)) > 0.0

        @pl.when(jnp.logical_not(all_distinct))
        def _redo_with_tie_breaks():
            _route_head(h, *refs, exact=True)

    zero = jnp.zeros((), BF16)
    packed = 2 * SUBLANES

    def row_bf16(ref, h, i1):
        words = jnp.broadcast_to(ref[h, pl.ds(i1, 1), :], (SUBLANES, t))
        return pltpu.bitcast(words, BF16)

    def expert_rows(i1, _):
        gates = [None] * (PEER_NKEYS // packed)
        for h in range(PEER_HEADS):
            thr, cf = row_bf16(thr_ref, h, i1), row_bf16(cf_ref, h, i1)
            for r in range(len(gates)):
                rs = slice(r * packed, (r + 1) * packed)
                g = cf * jnp.where(k1_ref[h, rs, :] >= thr, e1_ref[h, rs, :], zero)
                gates[r] = g if gates[r] is None else gates[r] + g
        for r, g in enumerate(gates):
            rows = pl.ds(pl.multiple_of(i1 * PEER_NKEYS + r * packed, packed), packed)
            for c, lt in enumerate(lane_tiles):
                g_ref[c, rows, :] = g[:, lt]
        return 0

    lax.fori_loop(0, PEER_NKEYS, expert_rows, 0)


def _peer_route(qt, subkeys, *, t=256):
    Q, T = qt.shape
    t = _tile(T, t)
    E = PEER_NKEYS * PEER_NKEYS
    scratch = pltpu.VMEM((PEER_HEADS, PEER_NKEYS, t), jnp.uint32)
    scratch_bf = pltpu.VMEM((PEER_HEADS, PEER_NKEYS, t), BF16)
    return pl.pallas_call(
        _route_kernel,
        out_shape=jax.ShapeDtypeStruct((T // LANES, E, LANES), BF16),
        grid=(T // t,),
        in_specs=[pl.BlockSpec((Q, t), lambda i: (0, i)),
                  pl.BlockSpec(subkeys.shape, lambda i: (0, 0, 0, 0))],
        out_specs=pl.BlockSpec((t // LANES, E, LANES), lambda i: (i, 0, 0)),
        scratch_shapes=[scratch, scratch, scratch_bf, scratch_bf],
        compiler_params=_cparams("parallel"),
        name="peer_route",
    )(qt, subkeys)


def _peer_kernel(hn_ref, u_ref, v_ref, g_ref, h_ref, o_ref):
    e = pl.program_id(1)

    @pl.when(e == 0)
    def _residual():
        o_ref[...] = h_ref[...]

    a = jnp.dot(u_ref[...], hn_ref[...], preferred_element_type=F32)
    cols = []
    for c in range(g_ref.shape[0]):
        ac = a[:, c * LANES:(c + 1) * LANES]
        gelu = 0.5 * ac * (1.0 + lax.erf(ac * (2.0 ** -0.5)))
        cols.append((gelu * g_ref[c].astype(F32)).astype(BF16))
    act = jnp.concatenate(cols, axis=1) if len(cols) > 1 else cols[0]
    o_ref[...] += lax.dot_general(act, v_ref[...], (((0,), (0,)), ((), ())),
                                  preferred_element_type=F32)


def _peer_experts(hn_t, u_bf, v_bf, g_tiles, h, *, tm=512, te=1024):
    D, T = hn_t.shape
    E = u_bf.shape[0]
    tm, te = _tile(T, tm), _tile(E, te)
    once = pl.Buffered(1)
    return pl.pallas_call(
        _peer_kernel,
        out_shape=jax.ShapeDtypeStruct((T, D), F32),
        grid=(T // tm, E // te),
        in_specs=[pl.BlockSpec((D, tm), lambda i, e: (0, i), pipeline_mode=once),
                  pl.BlockSpec((te, D), lambda i, e: (e, 0)),
                  pl.BlockSpec((te, D), lambda i, e: (e, 0)),
                  pl.BlockSpec((tm // LANES, te, LANES), lambda i, e: (i, e, 0)),
                  pl.BlockSpec((tm, D), lambda i, e: (i, 0), pipeline_mode=once)],
        out_specs=pl.BlockSpec((tm, D), lambda i, e: (i, 0), pipeline_mode=once),
        compiler_params=_cparams("parallel", "arbitrary"),
        name="peer_experts",
    )(hn_t, u_bf, v_bf, g_tiles, h)


def _rope_tables(S):
    half = HEAD_DIM // 2
    inv = ROPE_THETA ** (-jnp.arange(half, dtype=F32) / half)
    ang = jnp.arange(S).astype(F32)[:, None] * inv[None, :]
    cos, sin = jnp.cos(ang), jnp.sin(ang)
    return jnp.concatenate([cos, cos], axis=-1), jnp.concatenate([-sin, sin], axis=-1)


def kernel(x, norm_mix, w_in, pool_w, pool_scale, q_norm, k_norm, w_pool_out,
           w_attn_out, w_o, norm_ffn, peer_wq, peer_subkeys, peer_u, peer_v):
    B, S, D = x.shape
    T = B * S
    depth = w_in.shape[0]
    pool_width = POOL_GROUPS * pool_w.shape[-1]
    attn_width = w_attn_out.shape[1]
    n_heads = attn_width // HEAD_DIM
    in_width = w_in.shape[-1]
    assert in_width == pool_width + 3 * attn_width + 2 * D
    assert S % MOBA_BLOCK == 0 and T % LANES == 0
    q_col = pool_width // HEAD_DIM
    k_col = q_col + n_heads
    v_col = k_col + n_heads
    gp_col = pool_width + 3 * attn_width
    ga_col = gp_col + D
    cos, sin_signed = _rope_tables(S)

    h = x.reshape(T, D)
    for l in range(depth):
        xn = _rmsnorm_bf16(h, norm_mix[l], transpose=False)
        z = _matmul(xn, w_in[l].astype(BF16), name="in_proj")
        z3 = z.reshape(B, S, in_width)
        y_pool = _pool_mixer(z3, pool_w[l].astype(BF16), pool_scale[l])
        y_attn = _moba_attention(z3, cos, sin_signed, q_norm[l], k_norm[l], n_heads=n_heads,
                                 q_col=q_col, k_col=k_col, v_col=v_col).reshape(T, attn_width)
        merged = _merge(y_pool, y_attn, w_pool_out[l].astype(BF16), w_attn_out[l].astype(BF16), z,
                        gp_col=gp_col, ga_col=ga_col)
        h = _proj_residual(merged, w_o[l].astype(BF16), h)

        hn_t = _rmsnorm_bf16(h, norm_ffn[l], transpose=True)
        qt = _matmul(peer_wq[l].T.astype(BF16), hn_t, name="peer_query")
        g_tiles = _peer_route(qt, peer_subkeys[l])
        h = _peer_experts(hn_t, peer_u[l].astype(BF16), peer_v[l].astype(BF16), g_tiles, h)
    return h.reshape(B, S, D)
```

```python
import functools

import jax
import jax.numpy as jnp
from jax import lax
from jax.experimental import pallas as pl
from jax.experimental.pallas import tpu as pltpu

POOL_WINDOWS = (2, 4, 8, 16)
POOL_GROUPS = len(POOL_WINDOWS)
POOL_HALO = 16
HEAD_DIM = 128
MOBA_BLOCK = 256
MOBA_TOPK = 3
MOBA_CHUNK = 4
MOBA_HEADS_PER_STEP = 2
ROPE_THETA = 10000.0
PEER_HEADS = 8
PEER_NKEYS = 128
PEER_HALF = 128
PEER_TOPK = 16
EPS = 1e-6
NEG = -1e30
LOG2_E = 1.4426950408889634

LANES = 128
SUBLANES = 8
VMEM_LIMIT_BYTES = 56 * 1024 * 1024

F32 = jnp.float32
BF16 = jnp.bfloat16


def _cparams(*sem):
    return pltpu.CompilerParams(dimension_semantics=sem, vmem_limit_bytes=VMEM_LIMIT_BYTES)


def _tile(n, want):
    t = min(n, want)
    assert n % t == 0, (n, t)
    return t


def _rms_kernel(x_ref, g_ref, o_ref, *, transpose):
    x = x_ref[...]
    r = lax.rsqrt(jnp.mean(x * x, axis=-1, keepdims=True) + EPS)
    y = x * r * g_ref[...]
    if transpose:
        y = y.T
    o_ref[...] = y.astype(o_ref.dtype)


def _rmsnorm_bf16(x2d, g, *, transpose, tm=256):
    T, D = x2d.shape
    tm = _tile(T, tm)
    if transpose:
        out_shape = jax.ShapeDtypeStruct((D, T), BF16)
        out_spec = pl.BlockSpec((D, tm), lambda i: (0, i))
    else:
        out_shape = jax.ShapeDtypeStruct((T, D), BF16)
        out_spec = pl.BlockSpec((tm, D), lambda i: (i, 0))
    return pl.pallas_call(
        functools.partial(_rms_kernel, transpose=transpose),
        out_shape=out_shape,
        grid=(T // tm,),
        in_specs=[pl.BlockSpec((tm, D), lambda i: (i, 0)),
                  pl.BlockSpec((1, D), lambda i: (0, 0))],
        out_specs=out_spec,
        compiler_params=_cparams("parallel"),
        name="rmsnorm_cast",
    )(x2d, g.reshape(1, D))


def _mm_kernel(a_ref, b_ref, o_ref):
    o_ref[...] = jnp.dot(a_ref[...], b_ref[...], preferred_element_type=F32).astype(o_ref.dtype)


def _matmul(a, b, *, tm=1024, tn=1024, out_dtype=F32, name="matmul"):
    M, K = a.shape
    _, N = b.shape
    tm, tn = _tile(M, tm), _tile(N, tn)
    return pl.pallas_call(
        _mm_kernel,
        out_shape=jax.ShapeDtypeStruct((M, N), out_dtype),
        grid=(M // tm, N // tn),
        in_specs=[pl.BlockSpec((tm, K), lambda i, j: (i, 0)),
                  pl.BlockSpec((K, tn), lambda i, j: (0, j))],
        out_specs=pl.BlockSpec((tm, tn), lambda i, j: (i, j)),
        compiler_params=_cparams("parallel", "arbitrary"),
        name=name,
    )(a, b)


def _pool_kernel(x_ref, halo_ref, pw_ref, ps_ref, o_ref, *, ts, C):
    i = pl.program_id(1)
    pos = i * ts + lax.broadcasted_iota(jnp.int32, (ts, 1), 0)
    for g, w in enumerate(POOL_WINDOWS):
        x = x_ref[0, :, g * C:(g + 1) * C]
        halo = halo_ref[0, :, g * C:(g + 1) * C]
        halo = jnp.where(i == 0, 0.0, halo)
        s = jnp.concatenate([halo, x], axis=0)
        d = 1
        while d < w:
            s = s + pltpu.roll(s, d, axis=0)
            d *= 2
        s = s[POOL_HALO:]
        cnt = jnp.minimum(pos + 1, w).astype(F32)
        y = (s / cnt - x).astype(BF16)
        o = jnp.dot(y, pw_ref[g], preferred_element_type=F32) * ps_ref[:, g * C:(g + 1) * C]
        o_ref[:, g * C:(g + 1) * C] = o.astype(o_ref.dtype)


def _pool_mixer(z3, pool_w_bf, pool_scale, *, ts=512):
    B, S, _ = z3.shape
    G, C, _ = pool_w_bf.shape
    W = G * C
    ts = _tile(S, ts)
    hb = ts // POOL_HALO
    n_s = S // ts
    return pl.pallas_call(
        functools.partial(_pool_kernel, ts=ts, C=C),
        out_shape=jax.ShapeDtypeStruct((B * S, W), BF16),
        grid=(B, S // ts),
        in_specs=[pl.BlockSpec((1, ts, W), lambda b, i: (b, i, 0)),
                  pl.BlockSpec((1, POOL_HALO, W), lambda b, i: (b, jnp.maximum(i * hb - 1, 0), 0)),
                  pl.BlockSpec((G, C, C), lambda b, i: (0, 0, 0)),
                  pl.BlockSpec((1, W), lambda b, i: (0, 0))],
        out_specs=pl.BlockSpec((ts, W), lambda b, i: (b * n_s + i, 0)),
        compiler_params=_cparams("parallel", "parallel"),
        name="pool_mixer",
    )(z3, z3, pool_w_bf, pool_scale.reshape(1, W))


def _norm_rope(t, gain, cos, sin_signed):
    r = lax.rsqrt(jnp.mean(t * t, axis=-1, keepdims=True) + EPS)
    t = t * r * gain
    return t * cos + pltpu.roll(t, HEAD_DIM // 2, axis=1) * sin_signed


def _moba_kernel(zq_ref, zk_ref, zv_ref, cos_ref, sin_ref, qg_ref, kg_ref, o_ref,
                 kr_ref, vt_ref, km_ref, bias_ref, s_ref, *, nb, ch, hps):
    L = MOBA_BLOCK
    hd = HEAD_DIM
    i = pl.program_id(1)
    heads = [slice(hh * hd, (hh + 1) * hd) for hh in range(hps)]

    @pl.when(i == 0)
    def _prepare_keys_values():
        for hh, hs in enumerate(heads):
            for j in range(nb):
                rows = slice(j * L, (j + 1) * L)
                part = slice((j % ch) * L, (j % ch + 1) * L)
                k = _norm_rope(zk_ref[0, rows, hs], kg_ref[...], cos_ref[rows, :], sin_ref[rows, :])
                kr_ref[hh, j // ch, part, :] = k.astype(BF16)
                km_ref[hh, j:j + 1, :] = jnp.mean(k, axis=0, keepdims=True)
                vt_ref[hh, j // ch, :, part] = zv_ref[0, rows, hs].T.astype(BF16)

    q0 = pl.multiple_of(i * L, L)
    cos_q, sin_q = cos_ref[pl.ds(q0, L), :], sin_ref[pl.ds(q0, L), :]
    blk = lax.broadcasted_iota(jnp.int32, (nb, L), 0)
    qsb = []
    for hh, hs in enumerate(heads):
        qt = _norm_rope(zq_ref[0, :, hs], qg_ref[...], cos_q, sin_q).T
        qsb.append((qt * (hd ** -0.5 * LOG2_E)).astype(BF16))
        gate = jnp.dot(km_ref[hh], qt, preferred_element_type=F32,
                       precision=lax.Precision.HIGHEST)
        gate = jnp.where(blk < i, gate, NEG)
        bias = jnp.full((nb, L), NEG, F32)
        for _ in range(min(MOBA_TOPK, nb)):
            m = jnp.max(gate, axis=0, keepdims=True)
            idx = jnp.min(jnp.where(gate == m, blk, nb), axis=0, keepdims=True)
            hit = blk == idx
            bias = jnp.where(hit & (idx < i), 0.0, bias)
            gate = jnp.where(hit, -jnp.inf, gate)
        bias_ref[hh] = bias

    kpos = lax.broadcasted_iota(jnp.int32, (L, L), 0)
    qpos = lax.broadcasted_iota(jnp.int32, (L, L), 1)
    causal_bias = jnp.where(kpos <= qpos, 0.0, NEG)

    def fold(x, op):
        parts = [x[r * SUBLANES:(r + 1) * SUBLANES] for r in range(L // SUBLANES)]
        while len(parts) > 1:
            parts = [op(parts[k], parts[k + 1]) for k in range(0, len(parts), 2)]
        return parts[0]

    n_chunks = i // ch + 1

    def score_chunk(c, mx):
        out = []
        for hh in range(hps):
            s = jnp.dot(kr_ref[hh, c], qsb[hh], preferred_element_type=F32)
            m = mx[hh]
            for b in range(ch):
                j = c * ch + b
                bj = jnp.where(j == i, causal_bias, bias_ref[hh, pl.ds(j, 1), :])
                sb = s[b * L:(b + 1) * L] + bj
                s_ref[hh, pl.ds(pl.multiple_of(j * L, L), L), :] = sb
                m = jnp.maximum(m, fold(sb, jnp.maximum))
            out.append(m)
        return tuple(out)

    mx = lax.fori_loop(0, n_chunks, score_chunk,
                       tuple(jnp.full((SUBLANES, L), NEG, F32) for _ in range(hps)))
    mq = [jnp.max(m, axis=0, keepdims=True) for m in mx]

    def value_chunk(c, carry):
        out = []
        for hh in range(hps):
            ps, acc = carry[hh]
            for b in range(ch):
                j = c * ch + b
                p = jnp.exp2(s_ref[hh, pl.ds(pl.multiple_of(j * L, L), L), :] - mq[hh])
                ps = ps + fold(p, jnp.add)
                acc = acc + jnp.dot(vt_ref[hh, c, :, b * L:(b + 1) * L], p.astype(BF16),
                                    preferred_element_type=F32)
            out.append((ps, acc))
        return tuple(out)

    init = tuple((jnp.zeros((SUBLANES, L), F32), jnp.zeros((hd, L), F32)) for _ in range(hps))
    res = lax.fori_loop(0, n_chunks, value_chunk, init)
    for hs, (ps, acc) in zip(heads, res):
        o_ref[0, :, hs] = (acc / jnp.sum(ps, axis=0, keepdims=True)).T.astype(o_ref.dtype)


def _moba_attention(z3, cos, sin_signed, q_gain, k_gain, *, n_heads, q_col, k_col, v_col,
                    chunk=MOBA_CHUNK, hps=MOBA_HEADS_PER_STEP):
    B, S, _ = z3.shape
    L = MOBA_BLOCK
    nb = S // L
    hd = HEAD_DIM
    ch = chunk
    while nb % ch:
        ch //= 2
    assert n_heads % hps == 0 and q_col % hps == 0 and k_col % hps == 0 and v_col % hps == 0
    g = n_heads // hps
    w = hps * hd
    return pl.pallas_call(
        functools.partial(_moba_kernel, nb=nb, ch=ch, hps=hps),
        out_shape=jax.ShapeDtypeStruct((B, S, n_heads * hd), BF16),
        grid=(B * g, nb),
        in_specs=[pl.BlockSpec((1, L, w), lambda bg, i: (bg // g, i, q_col // hps + bg % g)),
                  pl.BlockSpec((1, S, w), lambda bg, i: (bg // g, 0, k_col // hps + bg % g)),
                  pl.BlockSpec((1, S, w), lambda bg, i: (bg // g, 0, v_col // hps + bg % g)),
                  pl.BlockSpec((S, hd), lambda bg, i: (0, 0)),
                  pl.BlockSpec((S, hd), lambda bg, i: (0, 0)),
                  pl.BlockSpec((1, hd), lambda bg, i: (0, 0)),
                  pl.BlockSpec((1, hd), lambda bg, i: (0, 0))],
        out_specs=pl.BlockSpec((1, L, w), lambda bg, i: (bg // g, i, bg % g)),
        scratch_shapes=[pltpu.VMEM((hps, nb // ch, ch * L, hd), BF16),
                        pltpu.VMEM((hps, nb // ch, hd, ch * L), BF16),
                        pltpu.VMEM((hps, nb, hd), F32),
                        pltpu.VMEM((hps, nb, L), F32),
                        pltpu.VMEM((hps, S, L), F32)],
        compiler_params=_cparams("parallel", "arbitrary"),
        name="moba_attention",
    )(z3, z3, z3, cos, sin_signed, q_gain.reshape(1, hd), k_gain.reshape(1, hd))


def _merge_kernel(yp_ref, ya_ref, wp_ref, wa_ref, gp_ref, ga_ref, o_ref):
    p = jnp.dot(yp_ref[...], wp_ref[...], preferred_element_type=F32)
    a = jnp.dot(ya_ref[...], wa_ref[...], preferred_element_type=F32)
    o_ref[...] = (jax.nn.sigmoid(gp_ref[...]) * p + jax.nn.sigmoid(ga_ref[...]) * a).astype(o_ref.dtype)


def _merge(y_pool, y_attn, wp, wa, z2, *, gp_col, ga_col, tm=512, tn=1024):
    T, K = y_pool.shape
    D = wp.shape[1]
    tm, tn = _tile(T, tm), _tile(D, tn)
    gp_blk, ga_blk = gp_col // tn, ga_col // tn
    return pl.pallas_call(
        _merge_kernel,
        out_shape=jax.ShapeDtypeStruct((T, D), BF16),
        grid=(T // tm, D // tn),
        in_specs=[pl.BlockSpec((tm, K), lambda i, j: (i, 0)),
                  pl.BlockSpec((tm, K), lambda i, j: (i, 0)),
                  pl.BlockSpec((K, tn), lambda i, j: (0, j)),
                  pl.BlockSpec((K, tn), lambda i, j: (0, j)),
                  pl.BlockSpec((tm, tn), lambda i, j: (i, gp_blk + j)),
                  pl.BlockSpec((tm, tn), lambda i, j: (i, ga_blk + j))],
        out_specs=pl.BlockSpec((tm, tn), lambda i, j: (i, j)),
        compiler_params=_cparams("parallel", "arbitrary"),
        name="gated_merge",
    )(y_pool, y_attn, wp, wa, z2, z2)


def _proj_res_kernel(m_ref, w_ref, x_ref, o_ref):
    o_ref[...] = x_ref[...] + jnp.dot(m_ref[...], w_ref[...], preferred_element_type=F32)


def _proj_residual(m, w, x2d, *, tm=1024, tn=1024):
    T, K = m.shape
    D = w.shape[1]
    tm, tn = _tile(T, tm), _tile(D, tn)
    return pl.pallas_call(
        _proj_res_kernel,
        out_shape=jax.ShapeDtypeStruct((T, D), F32),
        grid=(T // tm, D // tn),
        in_specs=[pl.BlockSpec((tm, K), lambda i, j: (i, 0)),
                  pl.BlockSpec((K, tn), lambda i, j: (0, j)),
                  pl.BlockSpec((tm, tn), lambda i, j: (i, j))],
        out_specs=pl.BlockSpec((tm, tn), lambda i, j: (i, j)),
        compiler_params=_cparams("parallel", "arbitrary"),
        name="out_proj_residual",
    )(m, w, x2d)


def _top16_rows(scores):
    n, t = scores[0].shape
    rows = lax.broadcasted_iota(jnp.int32, (n, t), 0).astype(F32)
    krow = lax.broadcasted_iota(jnp.int32, (PEER_TOPK, t), 0)

    def body(k, carry):
        out = []
        for work, sv, si in carry:
            m = jnp.max(work, axis=0, keepdims=True)
            idx = jnp.min(jnp.where(work == m, rows, float(n)), axis=0, keepdims=True)
            out.append((jnp.where(rows == idx, -jnp.inf, work),
                        jnp.where(krow == k, m, sv),
                        jnp.where(krow == k, idx, si)))
        return tuple(out)

    zeros = jnp.zeros((PEER_TOPK, t), F32)
    res = lax.fori_loop(0, PEER_TOPK, body, tuple((s, zeros, zeros) for s in scores))
    return [(sv, si) for _, sv, si in res]


def _top16_pairs(sorted_pairs):
    K = PEER_TOPK
    t = sorted_pairs[0][0].shape[1]
    k2 = lax.broadcasted_iota(jnp.int32, (K, t), 0).astype(F32)
    flat = (k2,) + tuple(k2[:K // 2] + float(k1 * K) for k1 in range(1, K))
    cmax = tuple(sv0[0:1, :] + sv1[0:1, :] for sv0, sv1 in sorted_pairs)

    def body(_, carry):
        out = []
        for (cand, cnt, z), cm in zip(carry, cmax):
            m = jnp.max(cand[0], axis=0, keepdims=True)
            for c in cand[1:]:
                m = jnp.maximum(m, jnp.max(c, axis=0, keepdims=True))
            idx = None
            for c, f in zip(cand, flat):
                fi = jnp.min(jnp.where(c == m, f, float(K * K)), axis=0, keepdims=True)
                idx = fi if idx is None else jnp.minimum(idx, fi)
            cand = tuple(jnp.where(f == idx, -jnp.inf, c) for c, f in zip(cand, flat))
            cnt = jnp.where(k2 == jnp.floor(idx * (1.0 / K)), cnt + 1.0, cnt)
            out.append((cand, cnt, z + jnp.exp(m - cm)))
        return tuple(out)

    init = tuple(((sv0[0:1, :] + sv1,) + tuple(sv0[k1:k1 + 1, :] + sv1[:K // 2] for k1 in range(1, K)),
                  jnp.zeros((K, t), F32), jnp.zeros((1, t), F32)) for sv0, sv1 in sorted_pairs)
    res = lax.fori_loop(0, K, body, init)
    return [(cnt, z) for _, cnt, z in res]


def _count_ge(x, bound):
    return jnp.sum(jnp.where(x >= bound, 1.0, 0.0), axis=0, keepdims=True)


def _sort_desc(xs):
    xs = list(xs)
    n = len(xs)
    size = 2
    while size <= n:
        stride = size // 2
        while stride >= 1:
            for a in range(n):
                b = a ^ stride
                if b > a:
                    hi, lo = jnp.maximum(xs[a], xs[b]), jnp.minimum(xs[a], xs[b])
                    xs[a], xs[b] = (hi, lo) if (a & size) == 0 else (lo, hi)
            stride //= 2
        size *= 2
    return xs


def _top16_values(s):
    K = PEER_TOPK
    lists = _sort_desc([s[g * SUBLANES:(g + 1) * SUBLANES] for g in range(s.shape[0] // SUBLANES)])
    assert len(lists) == K
    shift = SUBLANES // 2
    while shift >= 1:
        other = [pltpu.roll(x, shift, axis=0) for x in lists]
        merged = [jnp.maximum(lists[k], other[K - 1 - k]) for k in range(K)]
        stride = K // 2
        while stride >= 1:
            for a in range(K):
                b = a ^ stride
                if b > a:
                    merged[a], merged[b] = jnp.maximum(merged[a], merged[b]), jnp.minimum(merged[a], merged[b])
            stride //= 2
        lists = merged
        shift //= 2
    return jnp.concatenate([x[0:1, :] for x in lists], axis=0)


def _staircase_sums(sv0, sv1):
    K = PEER_TOPK
    t = sv0.shape[1]
    k1_all = lax.broadcasted_iota(jnp.int32, (K, t), 0)
    k1_low = k1_all[:K // 2]
    cols = [jnp.where(k1_all >= 2, sv0 + sv1[0:1, :], -jnp.inf)]
    for k2 in range(1, K // 3):
        rows_in = K // (k2 + 1)
        cols.append(jnp.where((k1_low >= 2) & (k1_low < rows_in), sv0[:K // 2] + sv1[k2:k2 + 1, :], -jnp.inf))
    return sv0[0:1, :] + sv1, sv0[1:2, :] + sv1[:K // 2], cols


def _top16_pairs_distinct(sorted_pairs):
    K = PEER_TOPK
    t = sorted_pairs[0][0].shape[1]
    k1_low = lax.broadcasted_iota(jnp.int32, (K // 2, t), 0)
    cmax = tuple(sv0[0:1, :] + sv1[0:1, :] for sv0, sv1 in sorted_pairs)

    def flatten(stairs):
        row0, row1, cols = stairs
        return (row0, row1) + tuple(cols)

    def body(_, carry):
        out = []
        for (cand, _, z), cm in zip(carry, cmax):
            m = jnp.max(cand[0], axis=0, keepdims=True)
            for c in cand[1:]:
                m = jnp.maximum(m, jnp.max(c, axis=0, keepdims=True))
            out.append((tuple(jnp.where(c == m, -jnp.inf, c) for c in cand), m, z + jnp.exp(m - cm)))
        return tuple(out)

    zero = jnp.zeros((1, t), F32)
    stairs = [_staircase_sums(sv0, sv1) for sv0, sv1 in sorted_pairs]
    res = lax.fori_loop(0, K, body, tuple((flatten(s), zero, zero) for s in stairs))
    out = []
    for (row0, row1, cols), (_, kth, z) in zip(stairs, res):
        ge = [jnp.where(c >= kth, 1.0, 0.0) for c in cols]
        low = ge[0][:K // 2]
        for g in ge[1:]:
            low = low + g
        low = jnp.where(k1_low == 0, _count_ge(row0, kth), jnp.where(k1_low == 1, _count_ge(row1, kth), low))
        cnt = jnp.concatenate([low, ge[0][K // 2:]], axis=0)
        distinct = jnp.sum(cnt, axis=0, keepdims=True) == float(K)
        out.append((cnt, z, distinct))
    return out


def _route_head(h, qt_ref, sub_ref, thr_ref, cf_ref, k1_ref, e1_ref, lane_tiles, exact):
    hi = lax.Precision.HIGHEST
    K = PEER_TOPK
    base = h * 2 * PEER_HALF
    scores = []
    for lt in lane_tiles:
        for half in range(2):
            rows = slice(base + half * PEER_HALF, base + (half + 1) * PEER_HALF)
            scores.append(jnp.dot(sub_ref[h, half], qt_ref[rows, lt],
                                  preferred_element_type=F32, precision=hi))
    n = len(lane_tiles)
    if exact:
        tops = _top16_rows(scores)
        svs = [sv for sv, _ in tops]
        stairs = [(cnt, z, None) for cnt, z in _top16_pairs([(svs[2 * c], svs[2 * c + 1]) for c in range(n)])]
        key_row = lax.broadcasted_iota(jnp.int32, (PEER_NKEYS, LANES), 0).astype(F32)
        match = [[key_row == si[k:k + 1, :] for k in range(K)] for _, si in tops]
    else:
        svs = [_top16_values(s) for s in scores]
        stairs = _top16_pairs_distinct([(svs[2 * c], svs[2 * c + 1]) for c in range(n)])
        match = [[s == sv[k:k + 1, :] for k in range(K)] for s, sv in zip(scores, svs)]
    ok = None
    for c, lt in enumerate(lane_tiles):
        s0, s1 = scores[2 * c], scores[2 * c + 1]
        sv0, sv1 = svs[2 * c], svs[2 * c + 1]
        cnt, z, distinct = stairs[c]
        paired = jnp.zeros_like(s0)
        key1 = jnp.full_like(s1, -float(K))
        for k in range(K):
            paired = jnp.where(match[2 * c][k], cnt[k:k + 1, :], paired)
            key1 = jnp.where(match[2 * c + 1][k], -float(k), key1)
        thr_ref[h, :, lt] = 1.0 - paired
        cf_ref[h, :, lt] = jnp.exp(s0 - sv0[0:1, :]) / z
        e1_ref[h, :, lt] = jnp.exp(s1 - sv1[0:1, :]).astype(e1_ref.dtype)
        k1_ref[h, :, lt] = key1.astype(k1_ref.dtype)
        if not exact:
            good = (distinct & (_count_ge(s0, sv0[K - 1:K, :]) == float(K))
                    & (_count_ge(s1, sv1[K - 1:K, :]) == float(K)))
            ok = good if ok is None else ok & good
    return ok


def _route_kernel(qt_ref, sub_ref, g_ref, thr_ref, cf_ref, k1_ref, e1_ref):
    t = qt_ref.shape[1]
    lane_tiles = [slice(c * LANES, (c + 1) * LANES) for c in range(t // LANES)]
    refs = (qt_ref, sub_ref, thr_ref, cf_ref, k1_ref, e1_ref, lane_tiles)

    for h in range(PEER_HEADS):
        ok = _route_head(h, *refs, exact=False)
        all_distinct = jnp.min(jnp.where(ok, 1.0, 0.0)) > 0.0

        @pl.when(jnp.logical_not(all_distinct))
        def _redo_with_tie_breaks():
            _route_head(h, *refs, exact=True)

    zero = jnp.zeros((), BF16)

    def expert_rows(i1, _):
        gate = None
        for h in range(PEER_HEADS):
            thr = thr_ref[h, pl.ds(i1, 1), :].astype(BF16)
            cf = cf_ref[h, pl.ds(i1, 1), :].astype(BF16)
            g = cf * jnp.where(k1_ref[h] >= thr, e1_ref[h], zero)
            gate = g if gate is None else gate + g
        rows = pl.ds(pl.multiple_of(i1 * PEER_NKEYS, PEER_NKEYS), PEER_NKEYS)
        for c, lt in enumerate(lane_tiles):
            g_ref[c, rows, :] = gate[:, lt]
        return 0

    lax.fori_loop(0, PEER_NKEYS, expert_rows, 0)


def _peer_route(qt, subkeys, *, t=256):
    Q, T = qt.shape
    t = _tile(T, t)
    E = PEER_NKEYS * PEER_NKEYS
    scratch = pltpu.VMEM((PEER_HEADS, PEER_NKEYS, t), F32)
    scratch_bf = pltpu.VMEM((PEER_HEADS, PEER_NKEYS, t), BF16)
    return pl.pallas_call(
        _route_kernel,
        out_shape=jax.ShapeDtypeStruct((T // LANES, E, LANES), BF16),
        grid=(T // t,),
        in_specs=[pl.BlockSpec((Q, t), lambda i: (0, i)),
                  pl.BlockSpec(subkeys.shape, lambda i: (0, 0, 0, 0))],
        out_specs=pl.BlockSpec((t // LANES, E, LANES), lambda i: (i, 0, 0)),
        scratch_shapes=[scratch, scratch, scratch_bf, scratch_bf],
        compiler_params=_cparams("parallel"),
        name="peer_route",
    )(qt, subkeys)


def _peer_kernel(hn_ref, u_ref, v_ref, g_ref, h_ref, o_ref):
    e = pl.program_id(1)

    @pl.when(e == 0)
    def _residual():
        o_ref[...] = h_ref[...]

    a = jnp.dot(u_ref[...], hn_ref[...], preferred_element_type=F32)
    cols = []
    for c in range(g_ref.shape[0]):
        ac = a[:, c * LANES:(c + 1) * LANES]
        gelu = 0.5 * ac * (1.0 + lax.erf(ac * (2.0 ** -0.5)))
        cols.append((gelu * g_ref[c].astype(F32)).astype(BF16))
    act = jnp.concatenate(cols, axis=1) if len(cols) > 1 else cols[0]
    o_ref[...] += lax.dot_general(act, v_ref[...], (((0,), (0,)), ((), ())),
                                  preferred_element_type=F32)


def _peer_experts(hn_t, u_bf, v_bf, g_tiles, h, *, tm=512, te=1024):
    D, T = hn_t.shape
    E = u_bf.shape[0]
    tm, te = _tile(T, tm), _tile(E, te)
    once = pl.Buffered(1)
    return pl.pallas_call(
        _peer_kernel,
        out_shape=jax.ShapeDtypeStruct((T, D), F32),
        grid=(T // tm, E // te),
        in_specs=[pl.BlockSpec((D, tm), lambda i, e: (0, i), pipeline_mode=once),
                  pl.BlockSpec((te, D), lambda i, e: (e, 0)),
                  pl.BlockSpec((te, D), lambda i, e: (e, 0)),
                  pl.BlockSpec((tm // LANES, te, LANES), lambda i, e: (i, e, 0)),
                  pl.BlockSpec((tm, D), lambda i, e: (i, 0), pipeline_mode=once)],
        out_specs=pl.BlockSpec((tm, D), lambda i, e: (i, 0), pipeline_mode=once),
        compiler_params=_cparams("parallel", "arbitrary"),
        name="peer_experts",
    )(hn_t, u_bf, v_bf, g_tiles, h)


def _rope_tables(S):
    half = HEAD_DIM // 2
    inv = ROPE_THETA ** (-jnp.arange(half, dtype=F32) / half)
    ang = jnp.arange(S).astype(F32)[:, None] * inv[None, :]
    cos, sin = jnp.cos(ang), jnp.sin(ang)
    return jnp.concatenate([cos, cos], axis=-1), jnp.concatenate([-sin, sin], axis=-1)


def kernel(x, norm_mix, w_in, pool_w, pool_scale, q_norm, k_norm, w_pool_out,
           w_attn_out, w_o, norm_ffn, peer_wq, peer_subkeys, peer_u, peer_v):
    B, S, D = x.shape
    T = B * S
    depth = w_in.shape[0]
    pool_width = POOL_GROUPS * pool_w.shape[-1]
    attn_width = w_attn_out.shape[1]
    n_heads = attn_width // HEAD_DIM
    in_width = w_in.shape[-1]
    assert in_width == pool_width + 3 * attn_width + 2 * D
    assert S % MOBA_BLOCK == 0 and T % LANES == 0
    q_col = pool_width // HEAD_DIM
    k_col = q_col + n_heads
    v_col = k_col + n_heads
    gp_col = pool_width + 3 * attn_width
    ga_col = gp_col + D
    cos, sin_signed = _rope_tables(S)

    h = x.reshape(T, D)
    for l in range(depth):
        xn = _rmsnorm_bf16(h, norm_mix[l], transpose=False)
        z = _matmul(xn, w_in[l].astype(BF16), name="in_proj")
        z3 = z.reshape(B, S, in_width)
        y_pool = _pool_mixer(z3, pool_w[l].astype(BF16), pool_scale[l])
        y_attn = _moba_attention(z3, cos, sin_signed, q_norm[l], k_norm[l], n_heads=n_heads,
                                 q_col=q_col, k_col=k_col, v_col=v_col).reshape(T, attn_width)
        merged = _merge(y_pool, y_attn, w_pool_out[l].astype(BF16), w_attn_out[l].astype(BF16), z,
                        gp_col=gp_col, ga_col=ga_col)
        h = _proj_residual(merged, w_o[l].astype(BF16), h)

        hn_t = _rmsnorm_bf16(h, norm_ffn[l], transpose=True)
        qt = _matmul(peer_wq[l].T.astype(BF16), hn_t, name="peer_query")
        g_tiles = _peer_route(qt, peer_subkeys[l])
        h = _peer_experts(hn_t, peer_u[l].astype(BF16), peer_v[l].astype(BF16), g_tiles, h)
    return h.reshape(B, S, D)
```

```python
import functools

import jax
import jax.numpy as jnp
from jax import lax
from jax.experimental import pallas as pl
from jax.experimental.pallas import tpu as pltpu

POOL_WINDOWS = (2, 4, 8, 16)
POOL_GROUPS = len(POOL_WINDOWS)
POOL_HALO = 16
HEAD_DIM = 128
MOBA_BLOCK = 256
MOBA_TOPK = 3
MOBA_CHUNK = 4
MOBA_HEADS_PER_STEP = 4
ROPE_THETA = 10000.0
PEER_HEADS = 8
PEER_NKEYS = 128
PEER_HALF = 128
PEER_TOPK = 16
EPS = 1e-6
NEG = -1e30
LOG2_E = 1.4426950408889634

LANES = 128
SUBLANES = 8
VMEM_LIMIT_BYTES = 56 * 1024 * 1024

F32 = jnp.float32
BF16 = jnp.bfloat16


def _cparams(*sem):
    return pltpu.CompilerParams(dimension_semantics=sem, vmem_limit_bytes=VMEM_LIMIT_BYTES)


def _tile(n, want):
    t = min(n, want)
    assert n % t == 0, (n, t)
    return t


def _rms_kernel(x_ref, g_ref, o_ref, *, transpose):
    x = x_ref[...]
    r = lax.rsqrt(jnp.mean(x * x, axis=-1, keepdims=True) + EPS)
    y = x * r * g_ref[...]
    if transpose:
        y = y.T
    o_ref[...] = y.astype(o_ref.dtype)


def _rmsnorm_bf16(x2d, g, *, transpose, tm=256):
    T, D = x2d.shape
    tm = _tile(T, tm)
    if transpose:
        out_shape = jax.ShapeDtypeStruct((D, T), BF16)
        out_spec = pl.BlockSpec((D, tm), lambda i: (0, i))
    else:
        out_shape = jax.ShapeDtypeStruct((T, D), BF16)
        out_spec = pl.BlockSpec((tm, D), lambda i: (i, 0))
    return pl.pallas_call(
        functools.partial(_rms_kernel, transpose=transpose),
        out_shape=out_shape,
        grid=(T // tm,),
        in_specs=[pl.BlockSpec((tm, D), lambda i: (i, 0)),
                  pl.BlockSpec((1, D), lambda i: (0, 0))],
        out_specs=out_spec,
        compiler_params=_cparams("parallel"),
        name="rmsnorm_cast",
    )(x2d, g.reshape(1, D))


def _mm_kernel(a_ref, b_ref, o_ref):
    o_ref[...] = jnp.dot(a_ref[...], b_ref[...], preferred_element_type=F32).astype(o_ref.dtype)


def _matmul(a, b, *, tm=1024, tn=1024, out_dtype=F32, name="matmul"):
    M, K = a.shape
    _, N = b.shape
    tm, tn = _tile(M, tm), _tile(N, tn)
    return pl.pallas_call(
        _mm_kernel,
        out_shape=jax.ShapeDtypeStruct((M, N), out_dtype),
        grid=(M // tm, N // tn),
        in_specs=[pl.BlockSpec((tm, K), lambda i, j: (i, 0)),
                  pl.BlockSpec((K, tn), lambda i, j: (0, j))],
        out_specs=pl.BlockSpec((tm, tn), lambda i, j: (i, j)),
        compiler_params=_cparams("parallel", "arbitrary"),
        name=name,
    )(a, b)


def _pool_kernel(x_ref, halo_ref, pw_ref, ps_ref, o_ref, *, ts, C):
    i = pl.program_id(1)
    pos = i * ts + lax.broadcasted_iota(jnp.int32, (ts, 1), 0)
    for g, w in enumerate(POOL_WINDOWS):
        x = x_ref[0, :, g * C:(g + 1) * C]
        halo = halo_ref[0, :, g * C:(g + 1) * C]
        halo = jnp.where(i == 0, 0.0, halo)
        s = jnp.concatenate([halo, x], axis=0)
        d = 1
        while d < w:
            s = s + pltpu.roll(s, d, axis=0)
            d *= 2
        s = s[POOL_HALO:]
        cnt = jnp.minimum(pos + 1, w).astype(F32)
        y = (s / cnt - x).astype(BF16)
        o = jnp.dot(y, pw_ref[g], preferred_element_type=F32) * ps_ref[:, g * C:(g + 1) * C]
        o_ref[:, g * C:(g + 1) * C] = o.astype(o_ref.dtype)


def _pool_mixer(z3, pool_w_bf, pool_scale, *, ts=512):
    B, S, _ = z3.shape
    G, C, _ = pool_w_bf.shape
    W = G * C
    ts = _tile(S, ts)
    hb = ts // POOL_HALO
    n_s = S // ts
    return pl.pallas_call(
        functools.partial(_pool_kernel, ts=ts, C=C),
        out_shape=jax.ShapeDtypeStruct((B * S, W), BF16),
        grid=(B, S // ts),
        in_specs=[pl.BlockSpec((1, ts, W), lambda b, i: (b, i, 0)),
                  pl.BlockSpec((1, POOL_HALO, W), lambda b, i: (b, jnp.maximum(i * hb - 1, 0), 0)),
                  pl.BlockSpec((G, C, C), lambda b, i: (0, 0, 0)),
                  pl.BlockSpec((1, W), lambda b, i: (0, 0))],
        out_specs=pl.BlockSpec((ts, W), lambda b, i: (b * n_s + i, 0)),
        compiler_params=_cparams("parallel", "parallel"),
        name="pool_mixer",
    )(z3, z3, pool_w_bf, pool_scale.reshape(1, W))


def _norm_rope(t, gain, cos, sin_signed):
    r = lax.rsqrt(jnp.mean(t * t, axis=-1, keepdims=True) + EPS)
    t = t * r * gain
    return t * cos + pltpu.roll(t, HEAD_DIM // 2, axis=1) * sin_signed


def _moba_kernel(zq_ref, zk_ref, zv_ref, cos_ref, sin_ref, qg_ref, kg_ref, o_ref,
                 kr_ref, vt_ref, km_ref, bias_ref, s_ref, *, nb, ch, hps):
    L = MOBA_BLOCK
    hd = HEAD_DIM
    i = pl.program_id(1)
    heads = [slice(hh * hd, (hh + 1) * hd) for hh in range(hps)]

    @pl.when(i == 0)
    def _prepare_keys_values():
        for hh, hs in enumerate(heads):
            for j in range(nb):
                rows = slice(j * L, (j + 1) * L)
                part = slice((j % ch) * L, (j % ch + 1) * L)
                k = _norm_rope(zk_ref[0, rows, hs], kg_ref[...], cos_ref[rows, :], sin_ref[rows, :])
                kr_ref[hh, j // ch, part, :] = k.astype(BF16)
                km_ref[hh, j:j + 1, :] = jnp.mean(k, axis=0, keepdims=True)
                vt_ref[hh, j // ch, :, part] = zv_ref[0, rows, hs].T.astype(BF16)

    q0 = pl.multiple_of(i * L, L)
    cos_q, sin_q = cos_ref[pl.ds(q0, L), :], sin_ref[pl.ds(q0, L), :]
    blk = lax.broadcasted_iota(jnp.int32, (nb, L), 0)
    qsb = []
    for hh, hs in enumerate(heads):
        qt = _norm_rope(zq_ref[0, :, hs], qg_ref[...], cos_q, sin_q).T
        qsb.append((qt * (hd ** -0.5 * LOG2_E)).astype(BF16))
        gate = jnp.dot(km_ref[hh], qt, preferred_element_type=F32,
                       precision=lax.Precision.HIGHEST)
        gate = jnp.where(blk < i, gate, NEG)
        bias = jnp.full((nb, L), NEG, F32)
        for _ in range(min(MOBA_TOPK, nb)):
            m = jnp.max(gate, axis=0, keepdims=True)
            idx = jnp.min(jnp.where(gate == m, blk, nb), axis=0, keepdims=True)
            hit = blk == idx
            bias = jnp.where(hit & (idx < i), 0.0, bias)
            gate = jnp.where(hit, -jnp.inf, gate)
        bias_ref[hh] = bias

    kpos = lax.broadcasted_iota(jnp.int32, (L, L), 0)
    qpos = lax.broadcasted_iota(jnp.int32, (L, L), 1)
    causal_bias = jnp.where(kpos <= qpos, 0.0, NEG)

    def fold(x, op):
        parts = [x[r * SUBLANES:(r + 1) * SUBLANES] for r in range(L // SUBLANES)]
        while len(parts) > 1:
            parts = [op(parts[k], parts[k + 1]) for k in range(0, len(parts), 2)]
        return parts[0]

    n_chunks = i // ch + 1

    def score_chunk(c, mx):
        out = []
        for hh in range(hps):
            s = jnp.dot(kr_ref[hh, c], qsb[hh], preferred_element_type=F32)
            m = mx[hh]
            for b in range(ch):
                j = c * ch + b
                bj = jnp.where(j == i, causal_bias, bias_ref[hh, pl.ds(j, 1), :])
                sb = s[b * L:(b + 1) * L] + bj
                s_ref[hh, pl.ds(pl.multiple_of(j * L, L), L), :] = sb
                m = jnp.maximum(m, fold(sb, jnp.maximum))
            out.append(m)
        return tuple(out)

    mx = lax.fori_loop(0, n_chunks, score_chunk,
                       tuple(jnp.full((SUBLANES, L), NEG, F32) for _ in range(hps)))
    mq = [jnp.max(m, axis=0, keepdims=True) for m in mx]

    def value_chunk(c, carry):
        out = []
        for hh in range(hps):
            ps, acc = carry[hh]
            for b in range(ch):
                j = c * ch + b
                p = jnp.exp2(s_ref[hh, pl.ds(pl.multiple_of(j * L, L), L), :] - mq[hh])
                ps = ps + fold(p, jnp.add)
                acc = acc + jnp.dot(vt_ref[hh, c, :, b * L:(b + 1) * L], p.astype(BF16),
                                    preferred_element_type=F32)
            out.append((ps, acc))
        return tuple(out)

    init = tuple((jnp.zeros((SUBLANES, L), F32), jnp.zeros((hd, L), F32)) for _ in range(hps))
    res = lax.fori_loop(0, n_chunks, value_chunk, init)
    for hs, (ps, acc) in zip(heads, res):
        o_ref[0, :, hs] = (acc / jnp.sum(ps, axis=0, keepdims=True)).T.astype(o_ref.dtype)


def _moba_attention(z3, cos, sin_signed, q_gain, k_gain, *, n_heads, q_col, k_col, v_col,
                    chunk=MOBA_CHUNK, hps=MOBA_HEADS_PER_STEP):
    B, S, _ = z3.shape
    L = MOBA_BLOCK
    nb = S // L
    hd = HEAD_DIM
    ch = chunk
    while nb % ch:
        ch //= 2
    assert n_heads % hps == 0 and q_col % hps == 0 and k_col % hps == 0 and v_col % hps == 0
    g = n_heads // hps
    w = hps * hd
    once = pl.Buffered(1)
    return pl.pallas_call(
        functools.partial(_moba_kernel, nb=nb, ch=ch, hps=hps),
        out_shape=jax.ShapeDtypeStruct((B, S, n_heads * hd), BF16),
        grid=(B * g, nb),
        in_specs=[pl.BlockSpec((1, L, w), lambda bg, i: (bg // g, i, q_col // hps + bg % g)),
                  pl.BlockSpec((1, S, w), lambda bg, i: (bg // g, 0, k_col // hps + bg % g), pipeline_mode=once),
                  pl.BlockSpec((1, S, w), lambda bg, i: (bg // g, 0, v_col // hps + bg % g), pipeline_mode=once),
                  pl.BlockSpec((S, hd), lambda bg, i: (0, 0), pipeline_mode=once),
                  pl.BlockSpec((S, hd), lambda bg, i: (0, 0), pipeline_mode=once),
                  pl.BlockSpec((1, hd), lambda bg, i: (0, 0)),
                  pl.BlockSpec((1, hd), lambda bg, i: (0, 0))],
        out_specs=pl.BlockSpec((1, L, w), lambda bg, i: (bg // g, i, bg % g)),
        scratch_shapes=[pltpu.VMEM((hps, nb // ch, ch * L, hd), BF16),
                        pltpu.VMEM((hps, nb // ch, hd, ch * L), BF16),
                        pltpu.VMEM((hps, nb, hd), F32),
                        pltpu.VMEM((hps, nb, L), F32),
                        pltpu.VMEM((hps, S, L), F32)],
        compiler_params=_cparams("parallel", "arbitrary"),
        name="moba_attention",
    )(z3, z3, z3, cos, sin_signed, q_gain.reshape(1, hd), k_gain.reshape(1, hd))


def _merge_kernel(yp_ref, ya_ref, wp_ref, wa_ref, gp_ref, ga_ref, o_ref):
    p = jnp.dot(yp_ref[...], wp_ref[...], preferred_element_type=F32)
    a = jnp.dot(ya_ref[...], wa_ref[...], preferred_element_type=F32)
    o_ref[...] = (jax.nn.sigmoid(gp_ref[...]) * p + jax.nn.sigmoid(ga_ref[...]) * a).astype(o_ref.dtype)


def _merge(y_pool, y_attn, wp, wa, z2, *, gp_col, ga_col, tm=512, tn=1024):
    T, K = y_pool.shape
    D = wp.shape[1]
    tm, tn = _tile(T, tm), _tile(D, tn)
    gp_blk, ga_blk = gp_col // tn, ga_col // tn
    return pl.pallas_call(
        _merge_kernel,
        out_shape=jax.ShapeDtypeStruct((T, D), BF16),
        grid=(T // tm, D // tn),
        in_specs=[pl.BlockSpec((tm, K), lambda i, j: (i, 0)),
                  pl.BlockSpec((tm, K), lambda i, j: (i, 0)),
                  pl.BlockSpec((K, tn), lambda i, j: (0, j)),
                  pl.BlockSpec((K, tn), lambda i, j: (0, j)),
                  pl.BlockSpec((tm, tn), lambda i, j: (i, gp_blk + j)),
                  pl.BlockSpec((tm, tn), lambda i, j: (i, ga_blk + j))],
        out_specs=pl.BlockSpec((tm, tn), lambda i, j: (i, j)),
        compiler_params=_cparams("parallel", "arbitrary"),
        name="gated_merge",
    )(y_pool, y_attn, wp, wa, z2, z2)


def _proj_res_kernel(m_ref, w_ref, x_ref, o_ref):
    o_ref[...] = x_ref[...] + jnp.dot(m_ref[...], w_ref[...], preferred_element_type=F32)


def _proj_residual(m, w, x2d, *, tm=1024, tn=1024):
    T, K = m.shape
    D = w.shape[1]
    tm, tn = _tile(T, tm), _tile(D, tn)
    return pl.pallas_call(
        _proj_res_kernel,
        out_shape=jax.ShapeDtypeStruct((T, D), F32),
        grid=(T // tm, D // tn),
        in_specs=[pl.BlockSpec((tm, K), lambda i, j: (i, 0)),
                  pl.BlockSpec((K, tn), lambda i, j: (0, j)),
                  pl.BlockSpec((tm, tn), lambda i, j: (i, j))],
        out_specs=pl.BlockSpec((tm, tn), lambda i, j: (i, j)),
        compiler_params=_cparams("parallel", "arbitrary"),
        name="out_proj_residual",
    )(m, w, x2d)


def _top16_rows(scores):
    n, t = scores[0].shape
    rows = lax.broadcasted_iota(jnp.int32, (n, t), 0).astype(F32)
    krow = lax.broadcasted_iota(jnp.int32, (PEER_TOPK, t), 0)

    def body(k, carry):
        out = []
        for work, sv, si in carry:
            m = jnp.max(work, axis=0, keepdims=True)
            idx = jnp.min(jnp.where(work == m, rows, float(n)), axis=0, keepdims=True)
            out.append((jnp.where(rows == idx, -jnp.inf, work),
                        jnp.where(krow == k, m, sv),
                        jnp.where(krow == k, idx, si)))
        return tuple(out)

    zeros = jnp.zeros((PEER_TOPK, t), F32)
    res = lax.fori_loop(0, PEER_TOPK, body, tuple((s, zeros, zeros) for s in scores))
    return [(sv, si) for _, sv, si in res]


def _top16_pairs(sorted_pairs):
    K = PEER_TOPK
    t = sorted_pairs[0][0].shape[1]
    k2 = lax.broadcasted_iota(jnp.int32, (K, t), 0).astype(F32)
    flat = (k2,) + tuple(k2[:K // 2] + float(k1 * K) for k1 in range(1, K))
    cmax = tuple(sv0[0:1, :] + sv1[0:1, :] for sv0, sv1 in sorted_pairs)

    def body(_, carry):
        out = []
        for (cand, cnt, z), cm in zip(carry, cmax):
            m = jnp.max(cand[0], axis=0, keepdims=True)
            for c in cand[1:]:
                m = jnp.maximum(m, jnp.max(c, axis=0, keepdims=True))
            idx = None
            for c, f in zip(cand, flat):
                fi = jnp.min(jnp.where(c == m, f, float(K * K)), axis=0, keepdims=True)
                idx = fi if idx is None else jnp.minimum(idx, fi)
            cand = tuple(jnp.where(f == idx, -jnp.inf, c) for c, f in zip(cand, flat))
            cnt = jnp.where(k2 == jnp.floor(idx * (1.0 / K)), cnt + 1.0, cnt)
            out.append((cand, cnt, z + jnp.exp(m - cm)))
        return tuple(out)

    init = tuple(((sv0[0:1, :] + sv1,) + tuple(sv0[k1:k1 + 1, :] + sv1[:K // 2] for k1 in range(1, K)),
                  jnp.zeros((K, t), F32), jnp.zeros((1, t), F32)) for sv0, sv1 in sorted_pairs)
    res = lax.fori_loop(0, K, body, init)
    return [(cnt, z) for _, cnt, z in res]


def _count_ge(x, bound):
    return jnp.sum(jnp.where(x >= bound, 1.0, 0.0), axis=0, keepdims=True)


def _sort_desc(xs):
    xs = list(xs)
    n = len(xs)
    size = 2
    while size <= n:
        stride = size // 2
        while stride >= 1:
            for a in range(n):
                b = a ^ stride
                if b > a:
                    hi, lo = jnp.maximum(xs[a], xs[b]), jnp.minimum(xs[a], xs[b])
                    xs[a], xs[b] = (hi, lo) if (a & size) == 0 else (lo, hi)
            stride //= 2
        size *= 2
    return xs


def _top16_values(s):
    K = PEER_TOPK
    lists = _sort_desc([s[g * SUBLANES:(g + 1) * SUBLANES] for g in range(s.shape[0] // SUBLANES)])
    assert len(lists) == K
    shift = SUBLANES // 2
    while shift >= 1:
        other = [pltpu.roll(x, shift, axis=0) for x in lists]
        merged = [jnp.maximum(lists[k], other[K - 1 - k]) for k in range(K)]
        stride = K // 2
        while stride >= 1:
            for a in range(K):
                b = a ^ stride
                if b > a:
                    merged[a], merged[b] = jnp.maximum(merged[a], merged[b]), jnp.minimum(merged[a], merged[b])
            stride //= 2
        lists = merged
        shift //= 2
    return jnp.concatenate([x[0:1, :] for x in lists], axis=0)


def _staircase_sums(sv0, sv1):
    K = PEER_TOPK
    t = sv0.shape[1]
    k1_all = lax.broadcasted_iota(jnp.int32, (K, t), 0)
    k1_low = k1_all[:K // 2]
    cols = [jnp.where(k1_all >= 2, sv0 + sv1[0:1, :], -jnp.inf)]
    for k2 in range(1, K // 3):
        rows_in = K // (k2 + 1)
        cols.append(jnp.where((k1_low >= 2) & (k1_low < rows_in), sv0[:K // 2] + sv1[k2:k2 + 1, :], -jnp.inf))
    return sv0[0:1, :] + sv1, sv0[1:2, :] + sv1[:K // 2], cols


def _top16_pairs_distinct(sorted_pairs):
    K = PEER_TOPK
    t = sorted_pairs[0][0].shape[1]
    k1_low = lax.broadcasted_iota(jnp.int32, (K // 2, t), 0)
    cmax = tuple(sv0[0:1, :] + sv1[0:1, :] for sv0, sv1 in sorted_pairs)

    def flatten(stairs):
        row0, row1, cols = stairs
        return (row0, row1) + tuple(cols)

    def body(_, carry):
        out = []
        for (cand, _, z), cm in zip(carry, cmax):
            m = jnp.max(cand[0], axis=0, keepdims=True)
            for c in cand[1:]:
                m = jnp.maximum(m, jnp.max(c, axis=0, keepdims=True))
            out.append((tuple(jnp.where(c == m, -jnp.inf, c) for c in cand), m, z + jnp.exp(m - cm)))
        return tuple(out)

    zero = jnp.zeros((1, t), F32)
    stairs = [_staircase_sums(sv0, sv1) for sv0, sv1 in sorted_pairs]
    res = lax.fori_loop(0, K, body, tuple((flatten(s), zero, zero) for s in stairs))
    out = []
    for (row0, row1, cols), (_, kth, z) in zip(stairs, res):
        ge = [jnp.where(c >= kth, 1.0, 0.0) for c in cols]
        low = ge[0][:K // 2]
        for g in ge[1:]:
            low = low + g
        low = jnp.where(k1_low == 0, _count_ge(row0, kth), jnp.where(k1_low == 1, _count_ge(row1, kth), low))
        cnt = jnp.concatenate([low, ge[0][K // 2:]], axis=0)
        distinct = jnp.sum(cnt, axis=0, keepdims=True) == float(K)
        out.append((cnt, z, distinct))
    return out


def _route_head(h, qt_ref, sub_ref, thr_ref, cf_ref, k1_ref, e1_ref, lane_tiles, exact):
    hi = lax.Precision.HIGHEST
    K = PEER_TOPK
    base = h * 2 * PEER_HALF
    scores = []
    for lt in lane_tiles:
        for half in range(2):
            rows = slice(base + half * PEER_HALF, base + (half + 1) * PEER_HALF)
            scores.append(jnp.dot(sub_ref[h, half], qt_ref[rows, lt],
                                  preferred_element_type=F32, precision=hi))
    n = len(lane_tiles)
    if exact:
        tops = _top16_rows(scores)
        svs = [sv for sv, _ in tops]
        stairs = [(cnt, z, None) for cnt, z in _top16_pairs([(svs[2 * c], svs[2 * c + 1]) for c in range(n)])]
        key_row = lax.broadcasted_iota(jnp.int32, (PEER_NKEYS, LANES), 0).astype(F32)
        match = [[key_row == si[k:k + 1, :] for k in range(K)] for _, si in tops]
    else:
        svs = [_top16_values(s) for s in scores]
        stairs = _top16_pairs_distinct([(svs[2 * c], svs[2 * c + 1]) for c in range(n)])
        match = [[s == sv[k:k + 1, :] for k in range(K)] for s, sv in zip(scores, svs)]
    ok = None
    for c, lt in enumerate(lane_tiles):
        s0, s1 = scores[2 * c], scores[2 * c + 1]
        sv0, sv1 = svs[2 * c], svs[2 * c + 1]
        cnt, z, distinct = stairs[c]
        paired = jnp.zeros_like(s0)
        key1 = jnp.full_like(s1, -float(K))
        for k in range(K):
            paired = jnp.where(match[2 * c][k], cnt[k:k + 1, :], paired)
            key1 = jnp.where(match[2 * c + 1][k], -float(k), key1)
        thr_ref[h, :, lt] = 1.0 - paired
        cf_ref[h, :, lt] = jnp.exp(s0 - sv0[0:1, :]) / z
        e1_ref[h, :, lt] = jnp.exp(s1 - sv1[0:1, :]).astype(e1_ref.dtype)
        k1_ref[h, :, lt] = key1.astype(k1_ref.dtype)
        if not exact:
            good = (distinct & (_count_ge(s0, sv0[K - 1:K, :]) == float(K))
                    & (_count_ge(s1, sv1[K - 1:K, :]) == float(K)))
            ok = good if ok is None else ok & good
    return ok


def _route_kernel(qt_ref, sub_ref, g_ref, thr_ref, cf_ref, k1_ref, e1_ref):
    t = qt_ref.shape[1]
    lane_tiles = [slice(c * LANES, (c + 1) * LANES) for c in range(t // LANES)]
    refs = (qt_ref, sub_ref, thr_ref, cf_ref, k1_ref, e1_ref, lane_tiles)

    for h in range(PEER_HEADS):
        ok = _route_head(h, *refs, exact=False)
        all_distinct = jnp.min(jnp.where(ok, 1.0, 0.0)) > 0.0

        @pl.when(jnp.logical_not(all_distinct))
        def _redo_with_tie_breaks():
            _route_head(h, *refs, exact=True)

    zero = jnp.zeros((), BF16)

    def expert_rows(i1, _):
        gate = None
        for h in range(PEER_HEADS):
            thr = thr_ref[h, pl.ds(i1, 1), :].astype(BF16)
            cf = cf_ref[h, pl.ds(i1, 1), :].astype(BF16)
            g = cf * jnp.where(k1_ref[h] >= thr, e1_ref[h], zero)
            gate = g if gate is None else gate + g
        rows = pl.ds(pl.multiple_of(i1 * PEER_NKEYS, PEER_NKEYS), PEER_NKEYS)
        for c, lt in enumerate(lane_tiles):
            g_ref[c, rows, :] = gate[:, lt]
        return 0

    lax.fori_loop(0, PEER_NKEYS, expert_rows, 0)


def _peer_route(qt, subkeys, *, t=256):
    Q, T = qt.shape
    t = _tile(T, t)
    E = PEER_NKEYS * PEER_NKEYS
    scratch = pltpu.VMEM((PEER_HEADS, PEER_NKEYS, t), F32)
    scratch_bf = pltpu.VMEM((PEER_HEADS, PEER_NKEYS, t), BF16)
    return pl.pallas_call(
        _route_kernel,
        out_shape=jax.ShapeDtypeStruct((T // LANES, E, LANES), BF16),
        grid=(T // t,),
        in_specs=[pl.BlockSpec((Q, t), lambda i: (0, i)),
                  pl.BlockSpec(subkeys.shape, lambda i: (0, 0, 0, 0))],
        out_specs=pl.BlockSpec((t // LANES, E, LANES), lambda i: (i, 0, 0)),
        scratch_shapes=[scratch, scratch, scratch_bf, scratch_bf],
        compiler_params=_cparams("parallel"),
        name="peer_route",
    )(qt, subkeys)


def _peer_kernel(hn_ref, u_ref, v_ref, g_ref, h_ref, o_ref):
    e = pl.program_id(1)

    @pl.when(e == 0)
    def _residual():
        o_ref[...] = h_ref[...]

    a = jnp.dot(u_ref[...], hn_ref[...], preferred_element_type=F32)
    cols = []
    for c in range(g_ref.shape[0]):
        ac = a[:, c * LANES:(c + 1) * LANES]
        gelu = 0.5 * ac * (1.0 + lax.erf(ac * (2.0 ** -0.5)))
        cols.append((gelu * g_ref[c].astype(F32)).astype(BF16))
    act = jnp.concatenate(cols, axis=1) if len(cols) > 1 else cols[0]
    o_ref[...] += lax.dot_general(act, v_ref[...], (((0,), (0,)), ((), ())),
                                  preferred_element_type=F32)


def _peer_experts(hn_t, u_bf, v_bf, g_tiles, h, *, tm=512, te=1024):
    D, T = hn_t.shape
    E = u_bf.shape[0]
    tm, te = _tile(T, tm), _tile(E, te)
    once = pl.Buffered(1)
    return pl.pallas_call(
        _peer_kernel,
        out_shape=jax.ShapeDtypeStruct((T, D), F32),
        grid=(T // tm, E // te),
        in_specs=[pl.BlockSpec((D, tm), lambda i, e: (0, i), pipeline_mode=once),
                  pl.BlockSpec((te, D), lambda i, e: (e, 0)),
                  pl.BlockSpec((te, D), lambda i, e: (e, 0)),
                  pl.BlockSpec((tm // LANES, te, LANES), lambda i, e: (i, e, 0)),
                  pl.BlockSpec((tm, D), lambda i, e: (i, 0), pipeline_mode=once)],
        out_specs=pl.BlockSpec((tm, D), lambda i, e: (i, 0), pipeline_mode=once),
        compiler_params=_cparams("parallel", "arbitrary"),
        name="peer_experts",
    )(hn_t, u_bf, v_bf, g_tiles, h)


def _rope_tables(S):
    half = HEAD_DIM // 2
    inv = ROPE_THETA ** (-jnp.arange(half, dtype=F32) / half)
    ang = jnp.arange(S).astype(F32)[:, None] * inv[None, :]
    cos, sin = jnp.cos(ang), jnp.sin(ang)
    return jnp.concatenate([cos, cos], axis=-1), jnp.concatenate([-sin, sin], axis=-1)


def kernel(x, norm_mix, w_in, pool_w, pool_scale, q_norm, k_norm, w_pool_out,
           w_attn_out, w_o, norm_ffn, peer_wq, peer_subkeys, peer_u, peer_v):
    B, S, D = x.shape
    T = B * S
    depth = w_in.shape[0]
    pool_width = POOL_GROUPS * pool_w.shape[-1]
    attn_width = w_attn_out.shape[1]
    n_heads = attn_width // HEAD_DIM
    in_width = w_in.shape[-1]
    assert in_width == pool_width + 3 * attn_width + 2 * D
    assert S % MOBA_BLOCK == 0 and T % LANES == 0
    q_col = pool_width // HEAD_DIM
    k_col = q_col + n_heads
    v_col = k_col + n_heads
    gp_col = pool_width + 3 * attn_width
    ga_col = gp_col + D
    cos, sin_signed = _rope_tables(S)

    h = x.reshape(T, D)
    for l in range(depth):
        xn = _rmsnorm_bf16(h, norm_mix[l], transpose=False)
        z = _matmul(xn, w_in[l].astype(BF16), name="in_proj")
        z3 = z.reshape(B, S, in_width)
        y_pool = _pool_mixer(z3, pool_w[l].astype(BF16), pool_scale[l])
        y_attn = _moba_attention(z3, cos, sin_signed, q_norm[l], k_norm[l], n_heads=n_heads,
                                 q_col=q_col, k_col=k_col, v_col=v_col).reshape(T, attn_width)
        merged = _merge(y_pool, y_attn, w_pool_out[l].astype(BF16), w_attn_out[l].astype(BF16), z,
                        gp_col=gp_col, ga_col=ga_col)
        h = _proj_residual(merged, w_o[l].astype(BF16), h)

        hn_t = _rmsnorm_bf16(h, norm_ffn[l], transpose=True)
        qt = _matmul(peer_wq[l].T.astype(BF16), hn_t, name="peer_query")
        g_tiles = _peer_route(qt, peer_subkeys[l])
        h = _peer_experts(hn_t, peer_u[l].astype(BF16), peer_v[l].astype(BF16), g_tiles, h)
    return h.reshape(B, S, D)
```

```python
import functools

import jax
import jax.numpy as jnp
from jax import lax
from jax.experimental import pallas as pl
from jax.experimental.pallas import tpu as pltpu

POOL_WINDOWS = (2, 4, 8, 16)
POOL_GROUPS = len(POOL_WINDOWS)
POOL_HALO = 16
HEAD_DIM = 128
MOBA_BLOCK = 256
MOBA_TOPK = 3
MOBA_CHUNK = 4
MOBA_HEADS_PER_STEP = 4
ROPE_THETA = 10000.0
PEER_HEADS = 8
PEER_NKEYS = 128
PEER_HALF = 128
PEER_TOPK = 16
EPS = 1e-6
NEG = -1e30
LOG2_E = 1.4426950408889634

LANES = 128
SUBLANES = 8
VMEM_LIMIT_BYTES = 56 * 1024 * 1024

F32 = jnp.float32
BF16 = jnp.bfloat16


def _cparams(*sem):
    return pltpu.CompilerParams(dimension_semantics=sem, vmem_limit_bytes=VMEM_LIMIT_BYTES)


def _tile(n, want):
    t = min(n, want)
    assert n % t == 0, (n, t)
    return t


def _rms_kernel(x_ref, g_ref, o_ref, *, transpose):
    x = x_ref[...]
    r = lax.rsqrt(jnp.mean(x * x, axis=-1, keepdims=True) + EPS)
    y = x * r * g_ref[...]
    if transpose:
        y = y.T
    o_ref[...] = y.astype(o_ref.dtype)


def _rmsnorm_bf16(x2d, g, *, transpose, tm=256):
    T, D = x2d.shape
    tm = _tile(T, tm)
    if transpose:
        out_shape = jax.ShapeDtypeStruct((D, T), BF16)
        out_spec = pl.BlockSpec((D, tm), lambda i: (0, i))
    else:
        out_shape = jax.ShapeDtypeStruct((T, D), BF16)
        out_spec = pl.BlockSpec((tm, D), lambda i: (i, 0))
    return pl.pallas_call(
        functools.partial(_rms_kernel, transpose=transpose),
        out_shape=out_shape,
        grid=(T // tm,),
        in_specs=[pl.BlockSpec((tm, D), lambda i: (i, 0)),
                  pl.BlockSpec((1, D), lambda i: (0, 0))],
        out_specs=out_spec,
        compiler_params=_cparams("parallel"),
        name="rmsnorm_cast",
    )(x2d, g.reshape(1, D))


def _norm_query_kernel(x_ref, g_ref, w_ref, hn_ref, q_ref):
    x = x_ref[...]
    r = lax.rsqrt(jnp.mean(x * x, axis=-1, keepdims=True) + EPS)
    hn_t = (x * r * g_ref[...]).T.astype(BF16)
    hn_ref[...] = hn_t
    q_ref[...] = jnp.dot(w_ref[...], hn_t, preferred_element_type=F32)


def _norm_query(x2d, g, wq_t, *, tm=256):
    T, D = x2d.shape
    Q = wq_t.shape[0]
    tm = _tile(T, tm)
    return pl.pallas_call(
        _norm_query_kernel,
        out_shape=(jax.ShapeDtypeStruct((D, T), BF16), jax.ShapeDtypeStruct((Q, T), F32)),
        grid=(T // tm,),
        in_specs=[pl.BlockSpec((tm, D), lambda i: (i, 0)),
                  pl.BlockSpec((1, D), lambda i: (0, 0)),
                  pl.BlockSpec((Q, D), lambda i: (0, 0), pipeline_mode=pl.Buffered(1))],
        out_specs=(pl.BlockSpec((D, tm), lambda i: (0, i)), pl.BlockSpec((Q, tm), lambda i: (0, i))),
        compiler_params=_cparams("parallel"),
        name="norm_peer_query",
    )(x2d, g.reshape(1, D), wq_t)


def _mm_kernel(a_ref, b_ref, o_ref):
    o_ref[...] = jnp.dot(a_ref[...], b_ref[...], preferred_element_type=F32).astype(o_ref.dtype)


def _matmul(a, b, *, tm=1024, tn=1024, out_dtype=F32, name="matmul"):
    M, K = a.shape
    _, N = b.shape
    tm, tn = _tile(M, tm), _tile(N, tn)
    return pl.pallas_call(
        _mm_kernel,
        out_shape=jax.ShapeDtypeStruct((M, N), out_dtype),
        grid=(M // tm, N // tn),
        in_specs=[pl.BlockSpec((tm, K), lambda i, j: (i, 0)),
                  pl.BlockSpec((K, tn), lambda i, j: (0, j))],
        out_specs=pl.BlockSpec((tm, tn), lambda i, j: (i, j)),
        compiler_params=_cparams("parallel", "arbitrary"),
        name=name,
    )(a, b)


def _pool_kernel(x_ref, halo_ref, pw_ref, ps_ref, o_ref, *, ts, C):
    i = pl.program_id(1)
    pos = i * ts + lax.broadcasted_iota(jnp.int32, (ts, 1), 0)
    for g, w in enumerate(POOL_WINDOWS):
        x = x_ref[0, :, g * C:(g + 1) * C]
        halo = halo_ref[0, :, g * C:(g + 1) * C]
        halo = jnp.where(i == 0, 0.0, halo)
        s = jnp.concatenate([halo, x], axis=0)
        d = 1
        while d < w:
            s = s + pltpu.roll(s, d, axis=0)
            d *= 2
        s = s[POOL_HALO:]
        cnt = jnp.minimum(pos + 1, w).astype(F32)
        y = (s / cnt - x).astype(BF16)
        o = jnp.dot(y, pw_ref[g], preferred_element_type=F32) * ps_ref[:, g * C:(g + 1) * C]
        o_ref[:, g * C:(g + 1) * C] = o.astype(o_ref.dtype)


def _pool_mixer(z3, pool_w_bf, pool_scale, *, ts=512):
    B, S, _ = z3.shape
    G, C, _ = pool_w_bf.shape
    W = G * C
    ts = _tile(S, ts)
    hb = ts // POOL_HALO
    n_s = S // ts
    return pl.pallas_call(
        functools.partial(_pool_kernel, ts=ts, C=C),
        out_shape=jax.ShapeDtypeStruct((B * S, W), BF16),
        grid=(B, S // ts),
        in_specs=[pl.BlockSpec((1, ts, W), lambda b, i: (b, i, 0)),
                  pl.BlockSpec((1, POOL_HALO, W), lambda b, i: (b, jnp.maximum(i * hb - 1, 0), 0)),
                  pl.BlockSpec((G, C, C), lambda b, i: (0, 0, 0)),
                  pl.BlockSpec((1, W), lambda b, i: (0, 0))],
        out_specs=pl.BlockSpec((ts, W), lambda b, i: (b * n_s + i, 0)),
        compiler_params=_cparams("parallel", "parallel"),
        name="pool_mixer",
    )(z3, z3, pool_w_bf, pool_scale.reshape(1, W))


def _norm_rope(t, gain, cos, sin_signed):
    r = lax.rsqrt(jnp.mean(t * t, axis=-1, keepdims=True) + EPS)
    t = t * r * gain
    return t * cos + pltpu.roll(t, HEAD_DIM // 2, axis=1) * sin_signed


def _moba_kernel(zq_ref, zk_ref, zv_ref, cos_ref, sin_ref, qg_ref, kg_ref, o_ref,
                 kr_ref, vt_ref, km_ref, bias_ref, s_ref, *, nb, ch, hps):
    L = MOBA_BLOCK
    hd = HEAD_DIM
    i = pl.program_id(1)
    heads = [slice(hh * hd, (hh + 1) * hd) for hh in range(hps)]

    @pl.when(i == 0)
    def _prepare_keys_values():
        for hh, hs in enumerate(heads):
            for j in range(nb):
                rows = slice(j * L, (j + 1) * L)
                part = slice((j % ch) * L, (j % ch + 1) * L)
                k = _norm_rope(zk_ref[0, rows, hs], kg_ref[...], cos_ref[rows, :], sin_ref[rows, :])
                kr_ref[hh, j // ch, part, :] = k.astype(BF16)
                km_ref[hh, j:j + 1, :] = jnp.mean(k, axis=0, keepdims=True)
                vt_ref[hh, j // ch, :, part] = zv_ref[0, rows, hs].T.astype(BF16)

    q0 = pl.multiple_of(i * L, L)
    cos_q, sin_q = cos_ref[pl.ds(q0, L), :], sin_ref[pl.ds(q0, L), :]
    blk = lax.broadcasted_iota(jnp.int32, (nb, L), 0)
    qsb = []
    for hh, hs in enumerate(heads):
        qt = _norm_rope(zq_ref[0, :, hs], qg_ref[...], cos_q, sin_q).T
        qsb.append((qt * (hd ** -0.5 * LOG2_E)).astype(BF16))
        gate = jnp.dot(km_ref[hh], qt, preferred_element_type=F32,
                       precision=lax.Precision.HIGHEST)
        gate = jnp.where(blk < i, gate, NEG)
        bias = jnp.full((nb, L), NEG, F32)
        for _ in range(min(MOBA_TOPK, nb)):
            m = jnp.max(gate, axis=0, keepdims=True)
            idx = jnp.min(jnp.where(gate == m, blk, nb), axis=0, keepdims=True)
            hit = blk == idx
            bias = jnp.where(hit & (idx < i), 0.0, bias)
            gate = jnp.where(hit, -jnp.inf, gate)
        bias_ref[hh] = bias

    kpos = lax.broadcasted_iota(jnp.int32, (L, L), 0)
    qpos = lax.broadcasted_iota(jnp.int32, (L, L), 1)
    causal_bias = jnp.where(kpos <= qpos, 0.0, NEG)

    def fold(x, op):
        parts = [x[r * SUBLANES:(r + 1) * SUBLANES] for r in range(L // SUBLANES)]
        while len(parts) > 1:
            parts = [op(parts[k], parts[k + 1]) for k in range(0, len(parts), 2)]
        return parts[0]

    n_chunks = i // ch + 1

    def score_chunk(c, mx):
        out = []
        for hh in range(hps):
            s = jnp.dot(kr_ref[hh, c], qsb[hh], preferred_element_type=F32)
            m = mx[hh]
            for b in range(ch):
                j = c * ch + b
                bj = jnp.where(j == i, causal_bias, bias_ref[hh, pl.ds(j, 1), :])
                sb = s[b * L:(b + 1) * L] + bj
                s_ref[hh, pl.ds(pl.multiple_of(j * L, L), L), :] = sb
                m = jnp.maximum(m, fold(sb, jnp.maximum))
            out.append(m)
        return tuple(out)

    mx = lax.fori_loop(0, n_chunks, score_chunk,
                       tuple(jnp.full((SUBLANES, L), NEG, F32) for _ in range(hps)))
    mq = [jnp.max(m, axis=0, keepdims=True) for m in mx]

    def value_chunk(c, carry):
        out = []
        for hh in range(hps):
            ps, acc = carry[hh]
            for b in range(ch):
                j = c * ch + b
                p = jnp.exp2(s_ref[hh, pl.ds(pl.multiple_of(j * L, L), L), :] - mq[hh])
                ps = ps + fold(p, jnp.add)
                acc = acc + jnp.dot(vt_ref[hh, c, :, b * L:(b + 1) * L], p.astype(BF16),
                                    preferred_element_type=F32)
            out.append((ps, acc))
        return tuple(out)

    init = tuple((jnp.zeros((SUBLANES, L), F32), jnp.zeros((hd, L), F32)) for _ in range(hps))
    res = lax.fori_loop(0, n_chunks, value_chunk, init)
    for hs, (ps, acc) in zip(heads, res):
        o_ref[0, :, hs] = (acc / jnp.sum(ps, axis=0, keepdims=True)).T.astype(o_ref.dtype)


def _moba_attention(z3, cos, sin_signed, q_gain, k_gain, *, n_heads, q_col, k_col, v_col,
                    chunk=MOBA_CHUNK, hps=MOBA_HEADS_PER_STEP):
    B, S, _ = z3.shape
    L = MOBA_BLOCK
    nb = S // L
    hd = HEAD_DIM
    ch = chunk
    while nb % ch:
        ch //= 2
    assert n_heads % hps == 0 and q_col % hps == 0 and k_col % hps == 0 and v_col % hps == 0
    g = n_heads // hps
    w = hps * hd
    once = pl.Buffered(1)
    return pl.pallas_call(
        functools.partial(_moba_kernel, nb=nb, ch=ch, hps=hps),
        out_shape=jax.ShapeDtypeStruct((B, S, n_heads * hd), BF16),
        grid=(B * g, nb),
        in_specs=[pl.BlockSpec((1, L, w), lambda bg, i: (bg // g, i, q_col // hps + bg % g)),
                  pl.BlockSpec((1, S, w), lambda bg, i: (bg // g, 0, k_col // hps + bg % g), pipeline_mode=once),
                  pl.BlockSpec((1, S, w), lambda bg, i: (bg // g, 0, v_col // hps + bg % g), pipeline_mode=once),
                  pl.BlockSpec((S, hd), lambda bg, i: (0, 0), pipeline_mode=once),
                  pl.BlockSpec((S, hd), lambda bg, i: (0, 0), pipeline_mode=once),
                  pl.BlockSpec((1, hd), lambda bg, i: (0, 0)),
                  pl.BlockSpec((1, hd), lambda bg, i: (0, 0))],
        out_specs=pl.BlockSpec((1, L, w), lambda bg, i: (bg // g, i, bg % g)),
        scratch_shapes=[pltpu.VMEM((hps, nb // ch, ch * L, hd), BF16),
                        pltpu.VMEM((hps, nb // ch, hd, ch * L), BF16),
                        pltpu.VMEM((hps, nb, hd), F32),
                        pltpu.VMEM((hps, nb, L), F32),
                        pltpu.VMEM((hps, S, L), F32)],
        compiler_params=_cparams("parallel", "arbitrary"),
        name="moba_attention",
    )(z3, z3, z3, cos, sin_signed, q_gain.reshape(1, hd), k_gain.reshape(1, hd))


def _merge_kernel(yp_ref, ya_ref, wp_ref, wa_ref, gp_ref, ga_ref, o_ref):
    p = jnp.dot(yp_ref[...], wp_ref[...], preferred_element_type=F32)
    a = jnp.dot(ya_ref[...], wa_ref[...], preferred_element_type=F32)
    o_ref[...] = (jax.nn.sigmoid(gp_ref[...]) * p + jax.nn.sigmoid(ga_ref[...]) * a).astype(o_ref.dtype)


def _merge(y_pool, y_attn, wp, wa, z2, *, gp_col, ga_col, tm=512, tn=1024):
    T, K = y_pool.shape
    D = wp.shape[1]
    tm, tn = _tile(T, tm), _tile(D, tn)
    gp_blk, ga_blk = gp_col // tn, ga_col // tn
    return pl.pallas_call(
        _merge_kernel,
        out_shape=jax.ShapeDtypeStruct((T, D), BF16),
        grid=(T // tm, D // tn),
        in_specs=[pl.BlockSpec((tm, K), lambda i, j: (i, 0)),
                  pl.BlockSpec((tm, K), lambda i, j: (i, 0)),
                  pl.BlockSpec((K, tn), lambda i, j: (0, j)),
                  pl.BlockSpec((K, tn), lambda i, j: (0, j)),
                  pl.BlockSpec((tm, tn), lambda i, j: (i, gp_blk + j)),
                  pl.BlockSpec((tm, tn), lambda i, j: (i, ga_blk + j))],
        out_specs=pl.BlockSpec((tm, tn), lambda i, j: (i, j)),
        compiler_params=_cparams("parallel", "arbitrary"),
        name="gated_merge",
    )(y_pool, y_attn, wp, wa, z2, z2)


def _proj_res_kernel(m_ref, w_ref, x_ref, o_ref):
    o_ref[...] = x_ref[...] + jnp.dot(m_ref[...], w_ref[...], preferred_element_type=F32)


def _proj_residual(m, w, x2d, *, tm=1024, tn=1024):
    T, K = m.shape
    D = w.shape[1]
    tm, tn = _tile(T, tm), _tile(D, tn)
    return pl.pallas_call(
        _proj_res_kernel,
        out_shape=jax.ShapeDtypeStruct((T, D), F32),
        grid=(T // tm, D // tn),
        in_specs=[pl.BlockSpec((tm, K), lambda i, j: (i, 0)),
                  pl.BlockSpec((K, tn), lambda i, j: (0, j)),
                  pl.BlockSpec((tm, tn), lambda i, j: (i, j))],
        out_specs=pl.BlockSpec((tm, tn), lambda i, j: (i, j)),
        compiler_params=_cparams("parallel", "arbitrary"),
        name="out_proj_residual",
    )(m, w, x2d)


def _top16_rows(scores):
    n, t = scores[0].shape
    rows = lax.broadcasted_iota(jnp.int32, (n, t), 0).astype(F32)
    krow = lax.broadcasted_iota(jnp.int32, (PEER_TOPK, t), 0)

    def body(k, carry):
        out = []
        for work, sv, si in carry:
            m = jnp.max(work, axis=0, keepdims=True)
            idx = jnp.min(jnp.where(work == m, rows, float(n)), axis=0, keepdims=True)
            out.append((jnp.where(rows == idx, -jnp.inf, work),
                        jnp.where(krow == k, m, sv),
                        jnp.where(krow == k, idx, si)))
        return tuple(out)

    zeros = jnp.zeros((PEER_TOPK, t), F32)
    res = lax.fori_loop(0, PEER_TOPK, body, tuple((s, zeros, zeros) for s in scores))
    return [(sv, si) for _, sv, si in res]


def _top16_pairs(sorted_pairs):
    K = PEER_TOPK
    t = sorted_pairs[0][0].shape[1]
    k2 = lax.broadcasted_iota(jnp.int32, (K, t), 0).astype(F32)
    flat = (k2,) + tuple(k2[:K // 2] + float(k1 * K) for k1 in range(1, K))
    cmax = tuple(sv0[0:1, :] + sv1[0:1, :] for sv0, sv1 in sorted_pairs)

    def body(_, carry):
        out = []
        for (cand, cnt, z), cm in zip(carry, cmax):
            m = jnp.max(cand[0], axis=0, keepdims=True)
            for c in cand[1:]:
                m = jnp.maximum(m, jnp.max(c, axis=0, keepdims=True))
            idx = None
            for c, f in zip(cand, flat):
                fi = jnp.min(jnp.where(c == m, f, float(K * K)), axis=0, keepdims=True)
                idx = fi if idx is None else jnp.minimum(idx, fi)
            cand = tuple(jnp.where(f == idx, -jnp.inf, c) for c, f in zip(cand, flat))
            cnt = jnp.where(k2 == jnp.floor(idx * (1.0 / K)), cnt + 1.0, cnt)
            out.append((cand, cnt, z + jnp.exp(m - cm)))
        return tuple(out)

    init = tuple(((sv0[0:1, :] + sv1,) + tuple(sv0[k1:k1 + 1, :] + sv1[:K // 2] for k1 in range(1, K)),
                  jnp.zeros((K, t), F32), jnp.zeros((1, t), F32)) for sv0, sv1 in sorted_pairs)
    res = lax.fori_loop(0, K, body, init)
    return [(cnt, z) for _, cnt, z in res]


def _count_ge(x, bound):
    return jnp.sum(jnp.where(x >= bound, 1.0, 0.0), axis=0, keepdims=True)


def _sort_desc(xs):
    xs = list(xs)
    n = len(xs)
    size = 2
    while size <= n:
        stride = size // 2
        while stride >= 1:
            for a in range(n):
                b = a ^ stride
                if b > a:
                    hi, lo = jnp.maximum(xs[a], xs[b]), jnp.minimum(xs[a], xs[b])
                    xs[a], xs[b] = (hi, lo) if (a & size) == 0 else (lo, hi)
            stride //= 2
        size *= 2
    return xs


def _top16_values(s):
    K = PEER_TOPK
    lists = _sort_desc([s[g * SUBLANES:(g + 1) * SUBLANES] for g in range(s.shape[0] // SUBLANES)])
    assert len(lists) == K
    shift = SUBLANES // 2
    while shift >= 1:
        other = [pltpu.roll(x, shift, axis=0) for x in lists]
        merged = [jnp.maximum(lists[k], other[K - 1 - k]) for k in range(K)]
        stride = K // 2
        while stride >= 1:
            for a in range(K):
                b = a ^ stride
                if b > a:
                    merged[a], merged[b] = jnp.maximum(merged[a], merged[b]), jnp.minimum(merged[a], merged[b])
            stride //= 2
        lists = merged
        shift //= 2
    return jnp.concatenate([x[0:1, :] for x in lists], axis=0)


def _staircase_sums(sv0, sv1):
    K = PEER_TOPK
    t = sv0.shape[1]
    k1_all = lax.broadcasted_iota(jnp.int32, (K, t), 0)
    k1_low = k1_all[:K // 2]
    cols = [jnp.where(k1_all >= 2, sv0 + sv1[0:1, :], -jnp.inf)]
    for k2 in range(1, K // 3):
        rows_in = K // (k2 + 1)
        cols.append(jnp.where((k1_low >= 2) & (k1_low < rows_in), sv0[:K // 2] + sv1[k2:k2 + 1, :], -jnp.inf))
    return sv0[0:1, :] + sv1, sv0[1:2, :] + sv1[:K // 2], cols


def _top16_pairs_distinct(sorted_pairs):
    K = PEER_TOPK
    t = sorted_pairs[0][0].shape[1]
    k1_low = lax.broadcasted_iota(jnp.int32, (K // 2, t), 0)
    cmax = tuple(sv0[0:1, :] + sv1[0:1, :] for sv0, sv1 in sorted_pairs)

    def flatten(stairs):
        row0, row1, cols = stairs
        return (row0, row1) + tuple(cols)

    def body(_, carry):
        out = []
        for (cand, _, z), cm in zip(carry, cmax):
            m = jnp.max(cand[0], axis=0, keepdims=True)
            for c in cand[1:]:
                m = jnp.maximum(m, jnp.max(c, axis=0, keepdims=True))
            out.append((tuple(jnp.where(c == m, -jnp.inf, c) for c in cand), m, z + jnp.exp(m - cm)))
        return tuple(out)

    zero = jnp.zeros((1, t), F32)
    stairs = [_staircase_sums(sv0, sv1) for sv0, sv1 in sorted_pairs]
    res = lax.fori_loop(0, K, body, tuple((flatten(s), zero, zero) for s in stairs))
    out = []
    for (row0, row1, cols), (_, kth, z) in zip(stairs, res):
        ge = [jnp.where(c >= kth, 1.0, 0.0) for c in cols]
        low = ge[0][:K // 2]
        for g in ge[1:]:
            low = low + g
        low = jnp.where(k1_low == 0, _count_ge(row0, kth), jnp.where(k1_low == 1, _count_ge(row1, kth), low))
        cnt = jnp.concatenate([low, ge[0][K // 2:]], axis=0)
        distinct = jnp.sum(cnt, axis=0, keepdims=True) == float(K)
        out.append((cnt, z, distinct))
    return out


def _route_head(h, qt_ref, sub_ref, thr_ref, cf_ref, k1_ref, e1_ref, lane_tiles, exact):
    hi = lax.Precision.HIGHEST
    K = PEER_TOPK
    base = h * 2 * PEER_HALF
    scores = []
    for lt in lane_tiles:
        for half in range(2):
            rows = slice(base + half * PEER_HALF, base + (half + 1) * PEER_HALF)
            scores.append(jnp.dot(sub_ref[h, half], qt_ref[rows, lt],
                                  preferred_element_type=F32, precision=hi))
    n = len(lane_tiles)
    if exact:
        tops = _top16_rows(scores)
        svs = [sv for sv, _ in tops]
        stairs = [(cnt, z, None) for cnt, z in _top16_pairs([(svs[2 * c], svs[2 * c + 1]) for c in range(n)])]
        key_row = lax.broadcasted_iota(jnp.int32, (PEER_NKEYS, LANES), 0).astype(F32)
        match = [[key_row == si[k:k + 1, :] for k in range(K)] for _, si in tops]
    else:
        svs = [_top16_values(s) for s in scores]
        stairs = _top16_pairs_distinct([(svs[2 * c], svs[2 * c + 1]) for c in range(n)])
        match = [[s == sv[k:k + 1, :] for k in range(K)] for s, sv in zip(scores, svs)]
    ok = None
    for c, lt in enumerate(lane_tiles):
        s0, s1 = scores[2 * c], scores[2 * c + 1]
        sv0, sv1 = svs[2 * c], svs[2 * c + 1]
        cnt, z, distinct = stairs[c]
        paired = jnp.zeros_like(s0)
        key1 = jnp.full_like(s1, -float(K))
        for k in range(K):
            paired = jnp.where(match[2 * c][k], cnt[k:k + 1, :], paired)
            key1 = jnp.where(match[2 * c + 1][k], -float(k), key1)
        thr_ref[h, :, lt] = 1.0 - paired
        cf_ref[h, :, lt] = jnp.exp(s0 - sv0[0:1, :]) / z
        e1_ref[h, :, lt] = jnp.exp(s1 - sv1[0:1, :]).astype(e1_ref.dtype)
        k1_ref[h, :, lt] = key1.astype(k1_ref.dtype)
        if not exact:
            good = (distinct & (_count_ge(s0, sv0[K - 1:K, :]) == float(K))
                    & (_count_ge(s1, sv1[K - 1:K, :]) == float(K)))
            ok = good if ok is None else ok & good
    return ok


def _route_kernel(qt_ref, sub_ref, g_ref, thr_ref, cf_ref, k1_ref, e1_ref):
    t = qt_ref.shape[1]
    lane_tiles = [slice(c * LANES, (c + 1) * LANES) for c in range(t // LANES)]
    refs = (qt_ref, sub_ref, thr_ref, cf_ref, k1_ref, e1_ref, lane_tiles)

    for h in range(PEER_HEADS):
        ok = _route_head(h, *refs, exact=False)
        all_distinct = jnp.min(jnp.where(ok, 1.0, 0.0)) > 0.0

        @pl.when(jnp.logical_not(all_distinct))
        def _redo_with_tie_breaks():
            _route_head(h, *refs, exact=True)

    zero = jnp.zeros((), BF16)

    def expert_rows(i1, _):
        gate = None
        for h in range(PEER_HEADS):
            thr = thr_ref[h, pl.ds(i1, 1), :].astype(BF16)
            cf = cf_ref[h, pl.ds(i1, 1), :].astype(BF16)
            g = cf * jnp.where(k1_ref[h] >= thr, e1_ref[h], zero)
            gate = g if gate is None else gate + g
        rows = pl.ds(pl.multiple_of(i1 * PEER_NKEYS, PEER_NKEYS), PEER_NKEYS)
        for c, lt in enumerate(lane_tiles):
            g_ref[c, rows, :] = gate[:, lt]
        return 0

    lax.fori_loop(0, PEER_NKEYS, expert_rows, 0)


def _peer_route(qt, subkeys, *, t=256):
    Q, T = qt.shape
    t = _tile(T, t)
    E = PEER_NKEYS * PEER_NKEYS
    scratch = pltpu.VMEM((PEER_HEADS, PEER_NKEYS, t), F32)
    scratch_bf = pltpu.VMEM((PEER_HEADS, PEER_NKEYS, t), BF16)
    return pl.pallas_call(
        _route_kernel,
        out_shape=jax.ShapeDtypeStruct((T // LANES, E, LANES), BF16),
        grid=(T // t,),
        in_specs=[pl.BlockSpec((Q, t), lambda i: (0, i)),
                  pl.BlockSpec(subkeys.shape, lambda i: (0, 0, 0, 0))],
        out_specs=pl.BlockSpec((t // LANES, E, LANES), lambda i: (i, 0, 0)),
        scratch_shapes=[scratch, scratch, scratch_bf, scratch_bf],
        compiler_params=_cparams("parallel"),
        name="peer_route",
    )(qt, subkeys)


def _peer_kernel(hn_ref, u_ref, v_ref, g_ref, h_ref, o_ref):
    e = pl.program_id(1)

    @pl.when(e == 0)
    def _residual():
        o_ref[...] = h_ref[...]

    a = jnp.dot(u_ref[...], hn_ref[...], preferred_element_type=F32)
    cols = []
    for c in range(g_ref.shape[0]):
        ac = a[:, c * LANES:(c + 1) * LANES]
        gelu = 0.5 * ac * (1.0 + lax.erf(ac * (2.0 ** -0.5)))
        cols.append((gelu * g_ref[c].astype(F32)).astype(BF16))
    act = jnp.concatenate(cols, axis=1) if len(cols) > 1 else cols[0]
    o_ref[...] += lax.dot_general(act, v_ref[...], (((0,), (0,)), ((), ())),
                                  preferred_element_type=F32)


def _peer_experts(hn_t, u_bf, v_bf, g_tiles, h, *, tm=512, te=1024):
    D, T = hn_t.shape
    E = u_bf.shape[0]
    tm, te = _tile(T, tm), _tile(E, te)
    once = pl.Buffered(1)
    return pl.pallas_call(
        _peer_kernel,
        out_shape=jax.ShapeDtypeStruct((T, D), F32),
        grid=(T // tm, E // te),
        in_specs=[pl.BlockSpec((D, tm), lambda i, e: (0, i), pipeline_mode=once),
                  pl.BlockSpec((te, D), lambda i, e: (e, 0)),
                  pl.BlockSpec((te, D), lambda i, e: (e, 0)),
                  pl.BlockSpec((tm // LANES, te, LANES), lambda i, e: (i, e, 0)),
                  pl.BlockSpec((tm, D), lambda i, e: (i, 0), pipeline_mode=once)],
        out_specs=pl.BlockSpec((tm, D), lambda i, e: (i, 0), pipeline_mode=once),
        compiler_params=_cparams("parallel", "arbitrary"),
        name="peer_experts",
    )(hn_t, u_bf, v_bf, g_tiles, h)


def _rope_tables(S):
    half = HEAD_DIM // 2
    inv = ROPE_THETA ** (-jnp.arange(half, dtype=F32) / half)
    ang = jnp.arange(S).astype(F32)[:, None] * inv[None, :]
    cos, sin = jnp.cos(ang), jnp.sin(ang)
    return jnp.concatenate([cos, cos], axis=-1), jnp.concatenate([-sin, sin], axis=-1)


def kernel(x, norm_mix, w_in, pool_w, pool_scale, q_norm, k_norm, w_pool_out,
           w_attn_out, w_o, norm_ffn, peer_wq, peer_subkeys, peer_u, peer_v):
    B, S, D = x.shape
    T = B * S
    depth = w_in.shape[0]
    pool_width = POOL_GROUPS * pool_w.shape[-1]
    attn_width = w_attn_out.shape[1]
    n_heads = attn_width // HEAD_DIM
    in_width = w_in.shape[-1]
    assert in_width == pool_width + 3 * attn_width + 2 * D
    assert S % MOBA_BLOCK == 0 and T % LANES == 0
    q_col = pool_width // HEAD_DIM
    k_col = q_col + n_heads
    v_col = k_col + n_heads
    gp_col = pool_width + 3 * attn_width
    ga_col = gp_col + D
    cos, sin_signed = _rope_tables(S)

    h = x.reshape(T, D)
    for l in range(depth):
        xn = _rmsnorm_bf16(h, norm_mix[l], transpose=False)
        z = _matmul(xn, w_in[l].astype(BF16), name="in_proj")
        z3 = z.reshape(B, S, in_width)
        y_pool = _pool_mixer(z3, pool_w[l].astype(BF16), pool_scale[l])
        y_attn = _moba_attention(z3, cos, sin_signed, q_norm[l], k_norm[l], n_heads=n_heads,
                                 q_col=q_col, k_col=k_col, v_col=v_col).reshape(T, attn_width)
        merged = _merge(y_pool, y_attn, w_pool_out[l].astype(BF16), w_attn_out[l].astype(BF16), z,
                        gp_col=gp_col, ga_col=ga_col)
        h = _proj_residual(merged, w_o[l].astype(BF16), h)

        hn_t, qt = _norm_query(h, norm_ffn[l], peer_wq[l].T.astype(BF16))
        g_tiles = _peer_route(qt, peer_subkeys[l])
        h = _peer_experts(hn_t, peer_u[l].astype(BF16), peer_v[l].astype(BF16), g_tiles, h)
    return h.reshape(B, S, D)
```
